```python
import jax, jax.numpy as jnp
from jax import lax
import numpy as np

D_MODEL = 1024
BATCH = 4
SEQ = 8192
DEPTH = 1

CONF_WIDTH = 1024
CONF_KERNEL = 31
GDN_HEADS = 8
GDN_HEAD_K = 128
GDN_HEAD_V = 128
GDN_KEY_WIDTH = GDN_HEADS * GDN_HEAD_K
GDN_VAL_WIDTH = GDN_HEADS * GDN_HEAD_V
GDN_CONV = 4
GDN_CHUNK = 64
LN_EPS = 1e-5
RMS_EPS = 1e-6
L2_EPS = 1e-6
DN_ALPHA = (2 * DEPTH) ** 0.25
DN_BETA = (8 * DEPTH) ** -0.25

IN_SPLITS = (CONF_WIDTH, CONF_WIDTH, CONF_WIDTH,
             GDN_KEY_WIDTH, GDN_KEY_WIDTH, GDN_VAL_WIDTH, GDN_VAL_WIDTH,
             GDN_HEADS, GDN_HEADS,
             D_MODEL, D_MODEL)
IN_WIDTH = sum(IN_SPLITS)

kernel_name = "hybrid_conformer_gdn_deepnorm"


def _split_cols(t, widths):
    out, start = [], 0
    for w in widths:
        out.append(t[..., start:start + w])
        start += w
    return out


def _layernorm(x, g, b):
    xf = x.astype(jnp.float32)
    mu = xf.mean(-1, keepdims=True)
    var = jnp.square(xf - mu).mean(-1, keepdims=True)
    y = (xf - mu) * lax.rsqrt(var + LN_EPS) * g.astype(jnp.float32) + b.astype(jnp.float32)
    return y.astype(x.dtype)


def _causal_depthwise_conv(x, w, b=None):
    K, C = w.shape
    y = lax.conv_general_dilated(
        x, w[:, None, :].astype(x.dtype), window_strides=(1,), padding=((K - 1, 0),),
        dimension_numbers=('NWC', 'WIO', 'NWC'), feature_group_count=C)
    if b is not None:
        y = y + b.astype(x.dtype)
    return y


def _chunk_gated_delta_rule(q, k, v, g, beta):
    B_, S, H, Dk = q.shape
    Dv = v.shape[-1]
    C = GDN_CHUNK
    N = S // C

    def chunk4(t):
        return jnp.moveaxis(t.reshape(B_, N, C, H, t.shape[-1]), 3, 2)

    def chunk3(t):
        return jnp.moveaxis(t.reshape(B_, N, C, H), 3, 2)

    q, k, v = chunk4(q), chunk4(k), chunk4(v)
    g, beta = chunk3(g), chunk3(beta)
    gc = jnp.cumsum(g, axis=-1)

    idx = jnp.arange(C)
    causal = idx[:, None] >= idx[None, :]
    strict = idx[:, None] > idx[None, :]
    decay = jnp.exp(jnp.where(causal, gc[..., :, None] - gc[..., None, :], -jnp.inf))

    kb = k * beta[..., None]
    L = jnp.where(strict, jnp.einsum('bnhik,bnhjk->bnhij', kb, k) * decay, 0.0)
    eye = jnp.eye(C, dtype=q.dtype)
    rhs = jnp.concatenate([v * beta[..., None], kb * jnp.exp(gc)[..., None]], axis=-1)
    sol = lax.linalg.triangular_solve(eye + L, rhs, left_side=True, lower=True, unit_diagonal=True)
    u, w = sol[..., :Dv], sol[..., Dv:]

    intra = jnp.where(causal, jnp.einsum('bnhik,bnhjk->bnhij', q, k) * decay, 0.0)
    q_dec = q * jnp.exp(gc)[..., None]
    g_last = gc[..., -1]
    k_dec = k * jnp.exp(g_last[..., None] - gc)[..., None]
    chunk_decay = jnp.exp(g_last)

    def step(state, inp):
        q_i, k_i, u_i, w_i, a_i, d_i = inp
        v_new = u_i - jnp.einsum('bhck,bhkv->bhcv', w_i, state)
        o = jnp.einsum('bhck,bhkv->bhcv', q_i, state) + jnp.einsum('bhij,bhjv->bhiv', a_i, v_new)
        state = state * d_i[..., None, None] + jnp.einsum('bhck,bhcv->bhkv', k_i, v_new)
        return state, o

    xs = tuple(jnp.moveaxis(t, 1, 0) for t in (q_dec, k_dec, u, w, intra, chunk_decay))
    state0 = jnp.zeros((B_, H, Dk, Dv), q.dtype)
    _, o = lax.scan(step, state0, xs)
    o = jnp.moveaxis(o, 0, 1)
    return jnp.moveaxis(o, 2, 3).reshape(B_, S, H, Dv)


def setup_inputs(seed: int = 0) -> dict:
    key = jax.random.key(seed)
    ks = jax.random.split(key, 16)
    f32 = jnp.float32
    x = jax.random.normal(ks[0], (BATCH, SEQ, D_MODEL), f32)
    col_scale = jnp.concatenate([
        jnp.full((w,), DN_BETA if i in (0, 5) else 1.0, f32) for i, w in enumerate(IN_SPLITS)])
    w_in = jax.random.normal(ks[1], (D_MODEL, IN_WIDTH), f32) * D_MODEL ** -0.5 * col_scale
    conf_dw_w = jax.random.normal(ks[2], (CONF_KERNEL, CONF_WIDTH), f32) * CONF_KERNEL ** -0.5
    conf_dw_b = 0.01 * jax.random.normal(ks[3], (CONF_WIDTH,), f32)
    conf_ln_g = 1.0 + 0.02 * jax.random.normal(ks[4], (CONF_WIDTH,), f32)
    conf_ln_b = 0.02 * jax.random.normal(ks[5], (CONF_WIDTH,), f32)
    conf_w_out = jax.random.normal(ks[6], (CONF_WIDTH, D_MODEL), f32) * CONF_WIDTH ** -0.5 * DN_BETA
    gdn_conv_w = jax.random.normal(ks[7], (GDN_CONV, 2 * GDN_KEY_WIDTH + GDN_VAL_WIDTH), f32) * GDN_CONV ** -0.5
    gdn_A_log = jnp.log(jax.random.uniform(ks[8], (GDN_HEADS,), f32, 1.0, 16.0))
    dt = jnp.exp(jax.random.uniform(ks[9], (GDN_HEADS,), f32, np.log(1e-3), np.log(1e-1)))
    gdn_dt_bias = dt + jnp.log(-jnp.expm1(-dt))
    gdn_norm_g = 1.0 + 0.02 * jax.random.normal(ks[10], (GDN_HEAD_V,), f32)
    gdn_w_out = jax.random.normal(ks[11], (GDN_VAL_WIDTH, D_MODEL), f32) * GDN_VAL_WIDTH ** -0.5 * DN_BETA
    w_o = jax.random.normal(ks[12], (D_MODEL, D_MODEL), f32) * D_MODEL ** -0.5 * DN_BETA
    post_ln_g = 1.0 + 0.02 * jax.random.normal(ks[13], (D_MODEL,), f32)
    post_ln_b = 0.02 * jax.random.normal(ks[14], (D_MODEL,), f32)
    return {"x": x, "w_in": w_in, "conf_dw_w": conf_dw_w, "conf_dw_b": conf_dw_b,
            "conf_ln_g": conf_ln_g, "conf_ln_b": conf_ln_b, "conf_w_out": conf_w_out,
            "gdn_conv_w": gdn_conv_w, "gdn_A_log": gdn_A_log, "gdn_dt_bias": gdn_dt_bias,
            "gdn_norm_g": gdn_norm_g, "gdn_w_out": gdn_w_out, "w_o": w_o,
            "post_ln_g": post_ln_g, "post_ln_b": post_ln_b}


def reference(x, w_in, conf_dw_w, conf_dw_b, conf_ln_g, conf_ln_b, conf_w_out,
              gdn_conv_w, gdn_A_log, gdn_dt_bias, gdn_norm_g, gdn_w_out, w_o,
              post_ln_g, post_ln_b):
    f32 = jnp.float32
    for _ in range(DEPTH):
        B_, S, _ = x.shape
        proj = jnp.einsum('bsd,de->bse', x, w_in)
        (c_val, c_glu, c_z, q, k, v, g_z, b_logit, a_logit,
         gate_c, gate_g) = _split_cols(proj, IN_SPLITS)

        a = c_val * jax.nn.sigmoid(c_glu)
        a = _causal_depthwise_conv(a, conf_dw_w, conf_dw_b)
        a = jax.nn.silu(_layernorm(a, conf_ln_g, conf_ln_b))
        y_conf = jnp.einsum('bsc,cd->bsd', a * jax.nn.silu(c_z), conf_w_out)

        qkv = jax.nn.silu(_causal_depthwise_conv(jnp.concatenate([q, k, v], axis=-1), gdn_conv_w))
        q, k, v = _split_cols(qkv, (GDN_KEY_WIDTH, GDN_KEY_WIDTH, GDN_VAL_WIDTH))
        q = q.reshape(B_, S, GDN_HEADS, GDN_HEAD_K).astype(f32)
        k = k.reshape(B_, S, GDN_HEADS, GDN_HEAD_K).astype(f32)
        v = v.reshape(B_, S, GDN_HEADS, GDN_HEAD_V).astype(f32)
        q = q * lax.rsqrt(jnp.sum(q * q, -1, keepdims=True) + L2_EPS) * (GDN_HEAD_K ** -0.5)
        k = k * lax.rsqrt(jnp.sum(k * k, -1, keepdims=True) + L2_EPS)
        beta = jax.nn.sigmoid(b_logit.astype(f32))
        g = -jnp.exp(gdn_A_log.astype(f32)) * jax.nn.softplus(a_logit.astype(f32) + gdn_dt_bias.astype(f32))
        o = _chunk_gated_delta_rule(q, k, v, g, beta)
        o = o * lax.rsqrt(jnp.mean(o * o, -1, keepdims=True) + RMS_EPS) * gdn_norm_g.astype(f32)
        o = o.reshape(B_, S, GDN_VAL_WIDTH).astype(x.dtype) * jax.nn.silu(g_z)
        y_gdn = jnp.einsum('bsc,cd->bsd', o, gdn_w_out)

        h = jax.nn.sigmoid(gate_c) * y_conf + jax.nn.sigmoid(gate_g) * y_gdn
        sub = jnp.einsum('bsd,de->bse', h, w_o)
        x = _layernorm(DN_ALPHA * x + sub, post_ln_g, post_ln_b)
    return x
```

```python
import functools

import jax
import jax.numpy as jnp
from jax import lax
from jax.experimental import pallas as pl
from jax.experimental.pallas import tpu as pltpu

D_MODEL = 1024
CONF_KERNEL = 31
GDN_HEADS = 8
GDN_HEAD_DIM = 128
GDN_CONV = 4
LN_EPS = 1e-5
RMS_EPS = 1e-6
L2_EPS = 1e-6
DN_ALPHA = 2.0 ** 0.25

LANES = 128
SUBLANES = 8
CONF_HALO = 32
GDN_HALO = 8
TIME_TILE = 256
CHUNK = 128
VMEM_LIMIT_BYTES = 60 * 1024 * 1024

BF16 = jnp.bfloat16
F32 = jnp.float32


def _dot(a, b):
    return jnp.dot(a.astype(BF16), b.astype(BF16), preferred_element_type=F32)


def _dot_nt(a, b):
    return lax.dot_general(a.astype(BF16), b.astype(BF16), (((1,), (1,)), ((), ())),
                           preferred_element_type=F32)


def _dot_tn(a, b):
    return lax.dot_general(a.astype(BF16), b.astype(BF16), (((0,), (0,)), ((), ())),
                           preferred_element_type=F32)


def _silu(x):
    return x * jax.nn.sigmoid(x)


def _layernorm(x, g, b):
    mu = jnp.mean(x, axis=-1, keepdims=True)
    xc = x - mu
    var = jnp.mean(xc * xc, axis=-1, keepdims=True)
    return xc * lax.rsqrt(var + LN_EPS) * g + b


def _causal_conv(buf_ref, w_ref, n_taps, halo, rows, col0, ncols):
    acc = None
    for k in range(n_taps):
        off = halo - (n_taps - 1) + k
        term = buf_ref[pl.ds(off, rows), pl.ds(col0, ncols)] * w_ref[pl.ds(k, 1), pl.ds(col0, ncols)]
        acc = term if acc is None else acc + term
    return acc


def _pre_kernel(x_ref, w_conf_ref, w_qkv_ref, w_gz_ref, w_ba_ref, w_gates_ref,
                dw_w_ref, dw_b_ref, ln_g_ref, ln_b_ref, conf_wo_ref, gconv_w_ref,
                alog_ref, dtb_ref,
                hc_ref, q_ref, k_ref, v_ref, gz_ref, gg_ref, bg_ref,
                abuf, cbuf):
    ts = x_ref.shape[1]
    t = pl.program_id(1)

    @pl.when(t == 0)
    def _():
        abuf[pl.ds(0, CONF_HALO), :] = jnp.zeros((CONF_HALO, D_MODEL), F32)
        cbuf[pl.ds(0, GDN_HALO), :] = jnp.zeros((GDN_HALO, 3 * D_MODEL), F32)

    xb = x_ref[0].astype(BF16)

    c_val = _dot(xb, w_conf_ref[:, pl.ds(0, D_MODEL)])
    c_glu = _dot(xb, w_conf_ref[:, pl.ds(D_MODEL, D_MODEL)])
    abuf[pl.ds(CONF_HALO, ts), :] = c_val * jax.nn.sigmoid(c_glu)
    conv = jnp.concatenate(
        [_causal_conv(abuf, dw_w_ref, CONF_KERNEL, CONF_HALO, ts, c * LANES, LANES)
         for c in range(D_MODEL // LANES)], axis=1) + dw_b_ref[...]
    abuf[pl.ds(0, CONF_HALO), :] = abuf[pl.ds(ts, CONF_HALO), :]
    act = _silu(_layernorm(conv, ln_g_ref[...], ln_b_ref[...]))
    c_z = _dot(xb, w_conf_ref[:, pl.ds(2 * D_MODEL, D_MODEL)])
    y_conf = _dot(act * _silu(c_z), conf_wo_ref[...])
    gate_c = _dot(xb, w_gates_ref[:, pl.ds(0, D_MODEL)])
    hc_ref[0] = (jax.nn.sigmoid(gate_c) * y_conf).astype(hc_ref.dtype)
    gate_g = _dot(xb, w_gates_ref[:, pl.ds(D_MODEL, D_MODEL)])
    gg_ref[0] = jax.nn.sigmoid(gate_g).astype(gg_ref.dtype)
    gz_ref[0] = _silu(_dot(xb, w_gz_ref[...])).astype(gz_ref.dtype)

    cbuf[pl.ds(GDN_HALO, ts), :] = _dot(xb, w_qkv_ref[...])
    outs = (q_ref, k_ref, v_ref)
    for part in range(3):
        for h in range(GDN_HEADS):
            col0 = part * D_MODEL + h * GDN_HEAD_DIM
            y = _silu(_causal_conv(cbuf, gconv_w_ref, GDN_CONV, GDN_HALO, ts, col0, GDN_HEAD_DIM))
            if part < 2:
                y = y * lax.rsqrt(jnp.sum(y * y, axis=-1, keepdims=True) + L2_EPS)
                if part == 0:
                    y = y * (GDN_HEAD_DIM ** -0.5)
            outs[part][0, :, pl.ds(h * GDN_HEAD_DIM, GDN_HEAD_DIM)] = y.astype(outs[part].dtype)
    cbuf[pl.ds(0, GDN_HALO), :] = cbuf[pl.ds(ts, GDN_HALO), :]

    ba = jnp.dot(xb, w_ba_ref[...], preferred_element_type=F32)
    z = ba + dtb_ref[...]
    softplus = jnp.maximum(z, 0.0) + jnp.log1p(jnp.exp(-jnp.abs(z)))
    g = -jnp.exp(alog_ref[...]) * softplus
    lane = lax.broadcasted_iota(jnp.int32, ba.shape, 1)
    bg_ref[0] = jnp.where(lane < GDN_HEADS, jax.nn.sigmoid(ba), g)


def _gdn_kernel(q_ref, k_ref, v_ref, bg_ref, gz_ref, gg_ref, hc_ref, x_ref,
                norm_g_ref, gdn_wo_ref, wo_ref, pln_g_ref, pln_b_ref,
                out_ref, state, obuf):
    ts = x_ref.shape[1]
    t = pl.program_id(1)
    C = CHUNK

    @pl.when(t == 0)
    def _():
        state[...] = jnp.zeros(state.shape, F32)

    row = lax.broadcasted_iota(jnp.int32, (C, C), 0)
    col = lax.broadcasted_iota(jnp.int32, (C, C), 1)
    causal = row >= col
    strict = row > col
    tri = causal.astype(F32)
    eye = (row == col).astype(F32)

    for c in range(ts // C):
        r0 = c * C
        bg = bg_ref[0, pl.ds(r0, C), :]
        gc = jnp.dot(tri, bg, preferred_element_type=F32, precision=lax.Precision.HIGHEST)
        gc_t = gc.T
        for h in range(GDN_HEADS):
            cols = pl.ds(h * GDN_HEAD_DIM, GDN_HEAD_DIM)
            qh = q_ref[0, pl.ds(r0, C), cols].astype(F32)
            kh = k_ref[0, pl.ds(r0, C), cols].astype(F32)
            vh = v_ref[0, pl.ds(r0, C), cols].astype(F32)
            beta = bg[:, h:h + 1]
            g_col = gc[:, GDN_HEADS + h:GDN_HEADS + h + 1]
            g_row = gc_t[GDN_HEADS + h:GDN_HEADS + h + 1, :]
            decay = jnp.exp(jnp.where(causal, g_col - g_row, 0.0))
            e_g = jnp.exp(g_col)
            kb = kh * beta
            kq = _dot_nt(jnp.concatenate([kb, qh], axis=0), kh)
            m = jnp.where(strict, -(kq[:C] * decay), 0.0)
            a_intra = jnp.where(causal, kq[C:] * decay, 0.0)
            inv = eye + m
            p = _dot(m, m)
            n_pow = 2
            while 2 * n_pow < C:
                r = _dot(jnp.concatenate([inv, p], axis=0), p)
                inv = inv + r[:C]
                p = r[C:]
                n_pow *= 2
            inv = inv + _dot(inv, p)
            uw = _dot(inv, jnp.concatenate([vh * beta, kb * e_g], axis=1))
            s = state[h]
            r2 = _dot(jnp.concatenate([uw[:, GDN_HEAD_DIM:], qh * e_g], axis=0), s)
            v_new = uw[:, :GDN_HEAD_DIM] - r2[:C]
            o = r2[C:] + _dot(a_intra, v_new)
            g_last = g_col[C - 1:C, :]
            k_dec = kh * jnp.exp(g_last - g_col)
            state[h] = s * jnp.exp(g_last) + _dot_tn(k_dec, v_new)
            o = o * lax.rsqrt(jnp.mean(o * o, axis=-1, keepdims=True) + RMS_EPS) * norm_g_ref[...]
            obuf[pl.ds(r0, C), cols] = o * gz_ref[0, pl.ds(r0, C), cols].astype(F32)

    y_gdn = _dot(obuf[...], gdn_wo_ref[...])
    hmix = hc_ref[0].astype(F32) + gg_ref[0].astype(F32) * y_gdn
    sub = _dot(hmix, wo_ref[...])
    out_ref[0] = _layernorm(DN_ALPHA * x_ref[0] + sub, pln_g_ref[...], pln_b_ref[...])


def _const_spec(shape):
    return pl.BlockSpec(shape, lambda b, t: (0,) * len(shape), pipeline_mode=pl.Buffered(1))


def _tile_spec(width):
    return pl.BlockSpec((1, TIME_TILE, width), lambda b, t: (b, t, 0))


def kernel(x, w_in, conf_dw_w, conf_dw_b, conf_ln_g, conf_ln_b, conf_w_out, gdn_conv_w, gdn_A_log,
           gdn_dt_bias, gdn_norm_g, gdn_w_out, w_o, post_ln_g, post_ln_b):
    B, S, D = x.shape
    assert D == D_MODEL and S % TIME_TILE == 0 and TIME_TILE % CHUNK == 0
    H = GDN_HEADS
    o_qkv, o_gz, o_ba, o_gates = 3 * D, 6 * D, 7 * D, 7 * D + 2 * H
    w_conf = w_in[:, :o_qkv].astype(BF16)
    w_qkv = w_in[:, o_qkv:o_gz].astype(BF16)
    w_gz = w_in[:, o_gz:o_ba].astype(BF16)
    w_ba = jnp.pad(w_in[:, o_ba:o_gates], ((0, 0), (0, LANES - 2 * H))).astype(BF16)
    w_gates = w_in[:, o_gates:].astype(BF16)
    dw_w = jnp.pad(conf_dw_w, ((0, CONF_HALO - CONF_KERNEL), (0, 0)))
    gconv_w = jnp.pad(gdn_conv_w, ((0, SUBLANES - GDN_CONV), (0, 0)))
    row = lambda v: v.reshape(1, -1).astype(F32)
    alog = jnp.pad(row(gdn_A_log), ((0, 0), (H, LANES - 2 * H)))
    dtb = jnp.pad(row(gdn_dt_bias), ((0, 0), (H, LANES - 2 * H)))

    grid = (B, S // TIME_TILE)
    params = pltpu.CompilerParams(dimension_semantics=("arbitrary", "arbitrary"),
                                  vmem_limit_bytes=VMEM_LIMIT_BYTES)
    act = lambda dt, w=D: jax.ShapeDtypeStruct((B, S, w), dt)

    hc, q, k, v, gz, gg, bg = pl.pallas_call(
        _pre_kernel,
        grid=grid,
        in_specs=[_tile_spec(D), _const_spec(w_conf.shape), _const_spec(w_qkv.shape),
                  _const_spec(w_gz.shape), _const_spec(w_ba.shape), _const_spec(w_gates.shape),
                  _const_spec(dw_w.shape), _const_spec((1, D)), _const_spec((1, D)),
                  _const_spec((1, D)), _const_spec((D, D)), _const_spec(gconv_w.shape),
                  _const_spec((1, LANES)), _const_spec((1, LANES))],
        out_specs=[_tile_spec(D)] * 6 + [_tile_spec(LANES)],
        out_shape=[act(BF16)] * 6 + [act(F32, LANES)],
        scratch_shapes=[pltpu.VMEM((TIME_TILE + CONF_HALO, D), F32),
                        pltpu.VMEM((TIME_TILE + GDN_HALO, 3 * D), F32)],
        compiler_params=params,
        name="pre",
    )(x, w_conf, w_qkv, w_gz, w_ba, w_gates, dw_w, row(conf_dw_b), row(conf_ln_g), row(conf_ln_b),
      conf_w_out.astype(BF16), gconv_w, alog, dtb)

    out = pl.pallas_call(
        _gdn_kernel,
        grid=grid,
        in_specs=[_tile_spec(D)] * 3 + [_tile_spec(LANES)] + [_tile_spec(D)] * 4
                 + [_const_spec((1, GDN_HEAD_DIM)), _const_spec((D, D)), _const_spec((D, D)),
                    _const_spec((1, D)), _const_spec((1, D))],
        out_specs=_tile_spec(D),
        out_shape=act(x.dtype),
        scratch_shapes=[pltpu.VMEM((H, GDN_HEAD_DIM, GDN_HEAD_DIM), F32),
                        pltpu.VMEM((TIME_TILE, D), F32)],
        compiler_params=params,
        name="gdn",
    )(q, k, v, bg, gz, gg, hc, x, row(gdn_norm_g), gdn_w_out.astype(BF16), w_o.astype(BF16),
      row(post_ln_g), row(post_ln_b))
    return out
```

```python
import functools

import jax
import jax.numpy as jnp
from jax import lax
from jax.experimental import pallas as pl
from jax.experimental.pallas import tpu as pltpu

D_MODEL = 1024
CONF_KERNEL = 31
GDN_HEADS = 8
GDN_HEAD_DIM = 128
GDN_CONV = 4
LN_EPS = 1e-5
RMS_EPS = 1e-6
L2_EPS = 1e-6
DN_ALPHA = 2.0 ** 0.25

LANES = 128
SUBLANES = 8
CONF_HALO = 32
GDN_HALO = 8
TIME_TILE = 256
CHUNK = 128
VMEM_LIMIT_BYTES = 60 * 1024 * 1024

BF16 = jnp.bfloat16
F32 = jnp.float32


def _dot(a, b):
    return jnp.dot(a.astype(BF16), b.astype(BF16), preferred_element_type=F32)


def _dot_nt(a, b):
    return lax.dot_general(a.astype(BF16), b.astype(BF16), (((1,), (1,)), ((), ())),
                           preferred_element_type=F32)


def _dot_tn(a, b):
    return lax.dot_general(a.astype(BF16), b.astype(BF16), (((0,), (0,)), ((), ())),
                           preferred_element_type=F32)


def _silu(x):
    return x * jax.nn.sigmoid(x)


def _layernorm(x, g, b):
    mu = jnp.mean(x, axis=-1, keepdims=True)
    xc = x - mu
    var = jnp.mean(xc * xc, axis=-1, keepdims=True)
    return xc * lax.rsqrt(var + LN_EPS) * g + b


def _causal_conv(buf_ref, w_ref, n_taps, halo, rows, col0, ncols):
    acc = None
    for k in range(n_taps):
        off = halo - (n_taps - 1) + k
        term = buf_ref[pl.ds(off, rows), pl.ds(col0, ncols)] * w_ref[pl.ds(k, 1), pl.ds(col0, ncols)]
        acc = term if acc is None else acc + term
    return acc


def _pre_kernel(x_ref, w_conf_ref, w_qkv_ref, w_gz_ref, w_ba_ref, w_gates_ref,
                dw_w_ref, dw_b_ref, ln_g_ref, ln_b_ref, conf_wo_ref, gconv_w_ref,
                alog_ref, dtb_ref,
                hc_ref, q_ref, k_ref, v_ref, gz_ref, gg_ref, bg_ref,
                abuf, cbuf):
    ts = x_ref.shape[1]
    t = pl.program_id(1)

    @pl.when(t == 0)
    def _():
        abuf[pl.ds(0, CONF_HALO), :] = jnp.zeros((CONF_HALO, D_MODEL), F32)
        cbuf[pl.ds(0, GDN_HALO), :] = jnp.zeros((GDN_HALO, 3 * D_MODEL), F32)

    xb = x_ref[0].astype(BF16)

    c_val = _dot(xb, w_conf_ref[:, pl.ds(0, D_MODEL)])
    c_glu = _dot(xb, w_conf_ref[:, pl.ds(D_MODEL, D_MODEL)])
    abuf[pl.ds(CONF_HALO, ts), :] = c_val * jax.nn.sigmoid(c_glu)
    conv = jnp.concatenate(
        [_causal_conv(abuf, dw_w_ref, CONF_KERNEL, CONF_HALO, ts, c * LANES, LANES)
         for c in range(D_MODEL // LANES)], axis=1) + dw_b_ref[...]
    abuf[pl.ds(0, CONF_HALO), :] = abuf[pl.ds(ts, CONF_HALO), :]
    act = _silu(_layernorm(conv, ln_g_ref[...], ln_b_ref[...]))
    c_z = _dot(xb, w_conf_ref[:, pl.ds(2 * D_MODEL, D_MODEL)])
    y_conf = _dot(act * _silu(c_z), conf_wo_ref[...])
    gate_c = _dot(xb, w_gates_ref[:, pl.ds(0, D_MODEL)])
    hc_ref[0] = (jax.nn.sigmoid(gate_c) * y_conf).astype(hc_ref.dtype)
    gate_g = _dot(xb, w_gates_ref[:, pl.ds(D_MODEL, D_MODEL)])
    gg_ref[0] = jax.nn.sigmoid(gate_g).astype(gg_ref.dtype)
    gz_ref[0] = _silu(_dot(xb, w_gz_ref[...])).astype(gz_ref.dtype)

    cbuf[pl.ds(GDN_HALO, ts), :] = _dot(xb, w_qkv_ref[...])
    outs = (q_ref, k_ref, v_ref)
    for part in range(3):
        for h in range(GDN_HEADS):
            col0 = part * D_MODEL + h * GDN_HEAD_DIM
            y = _silu(_causal_conv(cbuf, gconv_w_ref, GDN_CONV, GDN_HALO, ts, col0, GDN_HEAD_DIM))
            if part < 2:
                y = y * lax.rsqrt(jnp.sum(y * y, axis=-1, keepdims=True) + L2_EPS)
                if part == 0:
                    y = y * (GDN_HEAD_DIM ** -0.5)
            outs[part][0, :, pl.ds(h * GDN_HEAD_DIM, GDN_HEAD_DIM)] = y.astype(outs[part].dtype)
    cbuf[pl.ds(0, GDN_HALO), :] = cbuf[pl.ds(ts, GDN_HALO), :]

    ba = jnp.dot(xb, w_ba_ref[...], preferred_element_type=F32)
    z = ba + dtb_ref[...]
    softplus = jnp.maximum(z, 0.0) + jnp.log1p(jnp.exp(-jnp.abs(z)))
    g = -jnp.exp(alog_ref[...]) * softplus
    lane = lax.broadcasted_iota(jnp.int32, ba.shape, 1)
    bg_ref[0] = jnp.where(lane < GDN_HEADS, jax.nn.sigmoid(ba), g)


def _gdn_kernel(q_ref, k_ref, v_ref, bg_ref, gz_ref, gg_ref, hc_ref, x_ref,
                norm_g_ref, gdn_wo_ref, wo_ref, pln_g_ref, pln_b_ref,
                out_ref, state, obuf):
    ts = x_ref.shape[1]
    t = pl.program_id(1)
    C = CHUNK

    @pl.when(t == 0)
    def _():
        state[...] = jnp.zeros(state.shape, F32)

    row = lax.broadcasted_iota(jnp.int32, (C, C), 0)
    col = lax.broadcasted_iota(jnp.int32, (C, C), 1)
    causal = row >= col
    strict = row > col
    tri = causal.astype(F32)
    eye = (row == col).astype(F32)

    heads = range(GDN_HEADS)
    for c in range(ts // C):
        r0 = c * C
        bg = bg_ref[0, pl.ds(r0, C), :]
        gc = jnp.dot(tri, bg, preferred_element_type=F32, precision=lax.Precision.HIGHEST)
        gc_t = gc.T
        cols = [pl.ds(h * GDN_HEAD_DIM, GDN_HEAD_DIM) for h in heads]
        qs = [q_ref[0, pl.ds(r0, C), cols[h]].astype(F32) for h in heads]
        ks = [k_ref[0, pl.ds(r0, C), cols[h]].astype(F32) for h in heads]
        vs = [v_ref[0, pl.ds(r0, C), cols[h]].astype(F32) for h in heads]
        betas = [bg[:, h:h + 1] for h in heads]
        g_cols = [gc[:, GDN_HEADS + h:GDN_HEADS + h + 1] for h in heads]
        g_rows = [gc_t[GDN_HEADS + h:GDN_HEADS + h + 1, :] for h in heads]
        e_gs = [jnp.exp(g_cols[h]) for h in heads]
        kbs = [ks[h] * betas[h] for h in heads]
        kqs = [_dot_nt(jnp.concatenate([kbs[h], qs[h]], axis=0), ks[h]) for h in heads]
        decays = [jnp.exp(jnp.where(causal, g_cols[h] - g_rows[h], 0.0)) for h in heads]
        ms = [jnp.where(strict, -(kqs[h][:C] * decays[h]), 0.0) for h in heads]
        a_intras = [jnp.where(causal, kqs[h][C:] * decays[h], 0.0) for h in heads]
        invs = [eye + ms[h] for h in heads]
        ps = [_dot(ms[h], ms[h]) for h in heads]
        n_pow = 2
        while 2 * n_pow < C:
            rs = [_dot(jnp.concatenate([invs[h], ps[h]], axis=0), ps[h]) for h in heads]
            invs = [invs[h] + rs[h][:C] for h in heads]
            ps = [rs[h][C:] for h in heads]
            n_pow *= 2
        invs = [invs[h] + _dot(invs[h], ps[h]) for h in heads]
        uws = [_dot(invs[h], jnp.concatenate([vs[h] * betas[h], kbs[h] * e_gs[h]], axis=1))
               for h in heads]
        ss = [state[h] for h in heads]
        r2s = [_dot(jnp.concatenate([uws[h][:, GDN_HEAD_DIM:], qs[h] * e_gs[h]], axis=0), ss[h])
               for h in heads]
        v_news = [uws[h][:, :GDN_HEAD_DIM] - r2s[h][:C] for h in heads]
        os_ = [r2s[h][C:] + _dot(a_intras[h], v_news[h]) for h in heads]
        g_lasts = [g_cols[h][C - 1:C, :] for h in heads]
        k_decs = [ks[h] * jnp.exp(g_lasts[h] - g_cols[h]) for h in heads]
        for h in heads:
            state[h] = ss[h] * jnp.exp(g_lasts[h]) + _dot_tn(k_decs[h], v_news[h])
        for h in heads:
            o = os_[h]
            o = o * lax.rsqrt(jnp.mean(o * o, axis=-1, keepdims=True) + RMS_EPS) * norm_g_ref[...]
            obuf[pl.ds(r0, C), cols[h]] = o * gz_ref[0, pl.ds(r0, C), cols[h]].astype(F32)

    y_gdn = _dot(obuf[...], gdn_wo_ref[...])
    hmix = hc_ref[0].astype(F32) + gg_ref[0].astype(F32) * y_gdn
    sub = _dot(hmix, wo_ref[...])
    out_ref[0] = _layernorm(DN_ALPHA * x_ref[0] + sub, pln_g_ref[...], pln_b_ref[...])


def _const_spec(shape):
    return pl.BlockSpec(shape, lambda b, t: (0,) * len(shape), pipeline_mode=pl.Buffered(1))


def _tile_spec(width):
    return pl.BlockSpec((1, TIME_TILE, width), lambda b, t: (b, t, 0))


def kernel(x, w_in, conf_dw_w, conf_dw_b, conf_ln_g, conf_ln_b, conf_w_out, gdn_conv_w, gdn_A_log,
           gdn_dt_bias, gdn_norm_g, gdn_w_out, w_o, post_ln_g, post_ln_b):
    B, S, D = x.shape
    assert D == D_MODEL and S % TIME_TILE == 0 and TIME_TILE % CHUNK == 0
    H = GDN_HEADS
    o_qkv, o_gz, o_ba, o_gates = 3 * D, 6 * D, 7 * D, 7 * D + 2 * H
    w_conf = w_in[:, :o_qkv].astype(BF16)
    w_qkv = w_in[:, o_qkv:o_gz].astype(BF16)
    w_gz = w_in[:, o_gz:o_ba].astype(BF16)
    w_ba = jnp.pad(w_in[:, o_ba:o_gates], ((0, 0), (0, LANES - 2 * H))).astype(BF16)
    w_gates = w_in[:, o_gates:].astype(BF16)
    dw_w = jnp.pad(conf_dw_w, ((0, CONF_HALO - CONF_KERNEL), (0, 0)))
    gconv_w = jnp.pad(gdn_conv_w, ((0, SUBLANES - GDN_CONV), (0, 0)))
    row = lambda v: v.reshape(1, -1).astype(F32)
    alog = jnp.pad(row(gdn_A_log), ((0, 0), (H, LANES - 2 * H)))
    dtb = jnp.pad(row(gdn_dt_bias), ((0, 0), (H, LANES - 2 * H)))

    grid = (B, S // TIME_TILE)
    params = pltpu.CompilerParams(dimension_semantics=("arbitrary", "arbitrary"),
                                  vmem_limit_bytes=VMEM_LIMIT_BYTES)
    act = lambda dt, w=D: jax.ShapeDtypeStruct((B, S, w), dt)

    hc, q, k, v, gz, gg, bg = pl.pallas_call(
        _pre_kernel,
        grid=grid,
        in_specs=[_tile_spec(D), _const_spec(w_conf.shape), _const_spec(w_qkv.shape),
                  _const_spec(w_gz.shape), _const_spec(w_ba.shape), _const_spec(w_gates.shape),
                  _const_spec(dw_w.shape), _const_spec((1, D)), _const_spec((1, D)),
                  _const_spec((1, D)), _const_spec((D, D)), _const_spec(gconv_w.shape),
                  _const_spec((1, LANES)), _const_spec((1, LANES))],
        out_specs=[_tile_spec(D)] * 6 + [_tile_spec(LANES)],
        out_shape=[act(BF16)] * 6 + [act(F32, LANES)],
        scratch_shapes=[pltpu.VMEM((TIME_TILE + CONF_HALO, D), F32),
                        pltpu.VMEM((TIME_TILE + GDN_HALO, 3 * D), F32)],
        compiler_params=params,
        name="pre",
    )(x, w_conf, w_qkv, w_gz, w_ba, w_gates, dw_w, row(conf_dw_b), row(conf_ln_g), row(conf_ln_b),
      conf_w_out.astype(BF16), gconv_w, alog, dtb)

    out = pl.pallas_call(
        _gdn_kernel,
        grid=grid,
        in_specs=[_tile_spec(D)] * 3 + [_tile_spec(LANES)] + [_tile_spec(D)] * 4
                 + [_const_spec((1, GDN_HEAD_DIM)), _const_spec((D, D)), _const_spec((D, D)),
                    _const_spec((1, D)), _const_spec((1, D))],
        out_specs=_tile_spec(D),
        out_shape=act(x.dtype),
        scratch_shapes=[pltpu.VMEM((H, GDN_HEAD_DIM, GDN_HEAD_DIM), F32),
                        pltpu.VMEM((TIME_TILE, D), F32)],
        compiler_params=params,
        name="gdn",
    )(q, k, v, bg, gz, gg, hc, x, row(gdn_norm_g), gdn_w_out.astype(BF16), w_o.astype(BF16),
      row(post_ln_g), row(post_ln_b))
    return out
```

```python
import jax
import jax.numpy as jnp
from jax import lax
from jax.experimental import pallas as pl
from jax.experimental.pallas import tpu as pltpu

D_MODEL = 1024
CONF_KERNEL = 31
GDN_HEADS = 8
GDN_HEAD_DIM = 128
GDN_CONV = 4
LN_EPS = 1e-5
RMS_EPS = 1e-6
L2_EPS = 1e-6
DN_ALPHA = 2.0 ** 0.25

LANES = 128
SUBLANES = 8
MXU_N = 256
CONF_HALO = 32
GDN_HALO = 8
TIME_TILE = 256
CHUNK = 128
CONF_STEPS_WITH_PROJ = 10
VMEM_LIMIT_BYTES = 60 * 1024 * 1024

BF16 = jnp.bfloat16
F32 = jnp.float32

P_CZ, P_GATE_C, P_GATE_G, P_GZ = (i * D_MODEL for i in range(4))


def _dot(a, b):
    return jnp.dot(a.astype(BF16), b.astype(BF16), preferred_element_type=F32)


def _dot_nt(a, b):
    return lax.dot_general(a.astype(BF16), b.astype(BF16), (((1,), (1,)), ((), ())),
                           preferred_element_type=F32)


def _dot_tn(a, b):
    return lax.dot_general(a.astype(BF16), b.astype(BF16), (((0,), (0,)), ((), ())),
                           preferred_element_type=F32)


def _silu(x):
    return x * jax.nn.sigmoid(x)


def _layernorm(x, g, b):
    mu = jnp.mean(x, axis=-1, keepdims=True)
    xc = x - mu
    var = jnp.mean(xc * xc, axis=-1, keepdims=True)
    return xc * lax.rsqrt(var + LN_EPS) * g + b


def _shift_up(x, r, rows):
    if r == 0:
        return x[:rows]
    return pltpu.roll(x, x.shape[0] - r, axis=0)[:rows]


def _causal_conv(buf_ref, w_ref, n_taps, halo, row0, rows, col0):
    cols = pl.ds(col0, LANES)
    base = halo - (n_taps - 1)
    assert row0 % SUBLANES == 0 and rows % SUBLANES == 0
    acc = None
    for r in range(SUBLANES):
        offs = [o for o in range(base, base + n_taps) if o % SUBLANES == r]
        if not offs:
            continue
        n = rows if r == 0 else rows + SUBLANES
        part = None
        for o in offs:
            term = buf_ref[pl.ds(row0 + o - r, n), cols] * w_ref[pl.ds(o - base, 1), cols]
            part = term if part is None else part + term
        part = _shift_up(part, r, rows)
        acc = part if acc is None else acc + part
    return acc


def _conf_conv(buf_ref, w_ref, row0, rows, col0):
    return _causal_conv(buf_ref, w_ref, CONF_KERNEL, CONF_HALO, row0, rows, col0)


def _gdn_conv(buf_ref, w_ref, row0, rows, col0):
    return _causal_conv(buf_ref, w_ref, GDN_CONV, GDN_HALO, row0, rows, col0)


def _interleave(mxu_steps, vpu_steps):
    n_m, n_v = len(mxu_steps), len(vpu_steps)
    done_v = 0
    for i, step in enumerate(mxu_steps):
        step()
        upto = ((i + 1) * n_v) // n_m
        for j in range(done_v, upto):
            vpu_steps[j]()
        done_v = upto
    for j in range(done_v, n_v):
        vpu_steps[j]()


def _delta_rule_chunk(r0, qkvb, bgb, pbuf, state, obuf, norm_g_ref, masks, fill):
    C = CHUNK
    causal, strict, tri, eye = masks
    heads = range(GDN_HEADS)
    rows = pl.ds(r0, C)
    bg = bgb[rows, :]
    gc = jnp.dot(tri, bg, preferred_element_type=F32, precision=lax.Precision.HIGHEST)
    gc_t = gc.T
    cols = [pl.ds(h * GDN_HEAD_DIM, GDN_HEAD_DIM) for h in heads]
    qs = [qkvb[rows, pl.ds(h * GDN_HEAD_DIM, GDN_HEAD_DIM)] for h in heads]
    ks = [qkvb[rows, pl.ds(D_MODEL + h * GDN_HEAD_DIM, GDN_HEAD_DIM)] for h in heads]
    vs = [qkvb[rows, pl.ds(2 * D_MODEL + h * GDN_HEAD_DIM, GDN_HEAD_DIM)] for h in heads]
    betas = [bg[:, h:h + 1] for h in heads]
    g_cols = [gc[:, GDN_HEADS + h:GDN_HEADS + h + 1] for h in heads]
    g_rows = [gc_t[GDN_HEADS + h:GDN_HEADS + h + 1, :] for h in heads]
    e_gs = [jnp.exp(g_cols[h]) for h in heads]
    kbs = [ks[h] * betas[h] for h in heads]
    kqs = [_dot_nt(jnp.concatenate([kbs[h], qs[h]], axis=0), ks[h]) for h in heads]
    fill()
    decays = [jnp.exp(jnp.where(causal, g_cols[h] - g_rows[h], 0.0)) for h in heads]
    ms = [jnp.where(strict, -(kqs[h][:C] * decays[h]), 0.0) for h in heads]
    a_intras = [jnp.where(causal, kqs[h][C:] * decays[h], 0.0) for h in heads]
    invs = [eye + ms[h] for h in heads]
    ps = [_dot(ms[h], ms[h]) for h in heads]
    fill()
    n_pow = 2
    while 2 * n_pow < C:
        rs = [_dot(jnp.concatenate([invs[h], ps[h]], axis=0), ps[h]) for h in heads]
        fill()
        invs = [invs[h] + rs[h][:C] for h in heads]
        ps = [rs[h][C:] for h in heads]
        n_pow *= 2
    invs = [invs[h] + _dot(invs[h], ps[h]) for h in heads]
    fill()
    uws = [_dot(invs[h], jnp.concatenate([vs[h] * betas[h], kbs[h] * e_gs[h]], axis=1))
           for h in heads]
    fill()
    ss = [state[h] for h in heads]
    r2s = [_dot(jnp.concatenate([uws[h][:, GDN_HEAD_DIM:], qs[h] * e_gs[h]], axis=0), ss[h])
           for h in heads]
    fill()
    v_news = [uws[h][:, :GDN_HEAD_DIM] - r2s[h][:C] for h in heads]
    os_ = [r2s[h][C:] + _dot(a_intras[h], v_news[h]) for h in heads]
    fill()
    g_lasts = [g_cols[h][C - 1:C, :] for h in heads]
    k_decs = [ks[h] * jnp.exp(g_lasts[h] - g_cols[h]) for h in heads]
    for h in heads:
        state[h] = ss[h] * jnp.exp(g_lasts[h]) + _dot_tn(k_decs[h], v_news[h])
    fill()
    for h in heads:
        o = os_[h]
        o = o * lax.rsqrt(jnp.mean(o * o, axis=-1, keepdims=True) + RMS_EPS) * norm_g_ref[...]
        obuf[rows, cols[h]] = o * _silu(pbuf[rows, pl.ds(P_GZ + h * GDN_HEAD_DIM, GDN_HEAD_DIM)])


def _layer_kernel(x_ref, w_conf_ref, w_qkv_ref, w_gz_ref, w_ba_ref, w_gates_ref,
                  dw_w_ref, dw_b_ref, ln_g_ref, ln_b_ref, conf_wo_ref, gconv_w_ref,
                  alog_ref, dtb_ref, norm_g_ref, gdn_wo_ref, wo_ref, pln_g_ref, pln_b_ref,
                  out_ref,
                  xbuf, abuf, cbuf, convb, pbuf, qkvb, bgb, actb, obuf, state):
    ts = x_ref.shape[1]
    t = pl.program_id(1)
    n_tiles = D_MODEL // MXU_N

    @pl.when(t == 0)
    def _():
        abuf[pl.ds(0, CONF_HALO), :] = jnp.zeros((CONF_HALO, D_MODEL), F32)
        cbuf[pl.ds(0, GDN_HALO), :] = jnp.zeros((GDN_HALO, 3 * D_MODEL), F32)
        state[...] = jnp.zeros(state.shape, F32)

    xbuf[...] = x_ref[0].astype(BF16)

    def proj(w_ref, wcol):
        return jnp.dot(xbuf[...], w_ref[:, pl.ds(wcol, MXU_N)], preferred_element_type=F32)

    for j in range(n_tiles):
        c_val = proj(w_conf_ref, j * MXU_N)
        c_glu = proj(w_conf_ref, D_MODEL + j * MXU_N)
        abuf[pl.ds(CONF_HALO, ts), pl.ds(j * MXU_N, MXU_N)] = c_val * jax.nn.sigmoid(c_glu)

    def qkv_step(j):
        def step():
            cbuf[pl.ds(GDN_HALO, ts), pl.ds(j * MXU_N, MXU_N)] = proj(w_qkv_ref, j * MXU_N)
        return step

    def raw_step(w_ref, wcol, pcol):
        def step():
            pbuf[:, pl.ds(pcol, MXU_N)] = proj(w_ref, wcol)
        return step

    def conf_conv_step(row0, rows, c):
        def step():
            cols = pl.ds(c * LANES, LANES)
            convb[pl.ds(row0, rows), cols] = (_conf_conv(abuf, dw_w_ref, row0, rows, c * LANES)
                                              + dw_b_ref[:, cols])
        return step

    def gdn_conv_step(part, h):
        def step():
            col0 = part * D_MODEL + h * GDN_HEAD_DIM
            y = _silu(_gdn_conv(cbuf, gconv_w_ref, 0, ts, col0))
            if part < 2:
                y = y * lax.rsqrt(jnp.sum(y * y, axis=-1, keepdims=True) + L2_EPS)
                if part == 0:
                    y = y * (GDN_HEAD_DIM ** -0.5)
            qkvb[:, pl.ds(col0, GDN_HEAD_DIM)] = y
        return step

    def conf_act_step(row0, rows):
        def step():
            r = pl.ds(row0, rows)
            act = _silu(_layernorm(convb[r, :], ln_g_ref[...], ln_b_ref[...]))
            actb[r, :] = (act * _silu(pbuf[r, pl.ds(P_CZ, D_MODEL)])).astype(BF16)
        return step

    def conf_out_step(j):
        def step():
            cols = pl.ds(P_GATE_C + j * MXU_N, MXU_N)
            y_conf = jnp.dot(actb[...], conf_wo_ref[:, pl.ds(j * MXU_N, MXU_N)],
                             preferred_element_type=F32)
            pbuf[:, cols] = jax.nn.sigmoid(pbuf[:, cols]) * y_conf
        return step

    raw_steps = []
    for j in range(n_tiles):
        raw_steps.append(raw_step(w_conf_ref, 2 * D_MODEL + j * MXU_N, P_CZ + j * MXU_N))
    for j in range(n_tiles):
        raw_steps.append(raw_step(w_gates_ref, j * MXU_N, P_GATE_C + j * MXU_N))
    for j in range(n_tiles):
        raw_steps.append(raw_step(w_gates_ref, D_MODEL + j * MXU_N, P_GATE_G + j * MXU_N))
    for j in range(n_tiles):
        raw_steps.append(raw_step(w_gz_ref, j * MXU_N, P_GZ + j * MXU_N))
    qkv_steps = [qkv_step(j) for j in range(3 * n_tiles)]
    half = ts // 2
    conf_steps = [conf_conv_step(r0, half, c) for r0 in (0, half) for c in range(D_MODEL // LANES)]
    gdn_steps = [gdn_conv_step(part, h) for part in range(3) for h in range(GDN_HEADS)]
    n_early = CONF_STEPS_WITH_PROJ
    _interleave(qkv_steps + raw_steps, conf_steps[:n_early] + gdn_steps)

    ba = jnp.dot(xbuf[...], w_ba_ref[...], preferred_element_type=F32)
    z = ba + dtb_ref[...]
    softplus = jnp.maximum(z, 0.0) + jnp.log1p(jnp.exp(-jnp.abs(z)))
    g = -jnp.exp(alog_ref[...]) * softplus
    lane = lax.broadcasted_iota(jnp.int32, ba.shape, 1)
    bgb[...] = jnp.where(lane < GDN_HEADS, jax.nn.sigmoid(ba), g)

    pending = (conf_steps[n_early:] + [conf_act_step(r0, ts // 4) for r0 in range(0, ts, ts // 4)]
               + [conf_out_step(j) for j in range(n_tiles)])

    def fill():
        if pending:
            pending.pop(0)()

    C = CHUNK
    row = lax.broadcasted_iota(jnp.int32, (C, C), 0)
    col = lax.broadcasted_iota(jnp.int32, (C, C), 1)
    causal = row >= col
    masks = (causal, row > col, causal.astype(F32), (row == col).astype(F32))
    for c in range(ts // C):
        _delta_rule_chunk(c * C, qkvb, bgb, pbuf, state, obuf, norm_g_ref, masks, fill)
    while pending:
        fill()
    abuf[pl.ds(0, CONF_HALO), :] = abuf[pl.ds(ts, CONF_HALO), :]
    cbuf[pl.ds(0, GDN_HALO), :] = cbuf[pl.ds(ts, GDN_HALO), :]

    y_gdn = _dot(obuf[...], gdn_wo_ref[...])
    hmix = pbuf[:, pl.ds(P_GATE_C, D_MODEL)] + jax.nn.sigmoid(pbuf[:, pl.ds(P_GATE_G, D_MODEL)]) * y_gdn
    sub = _dot(hmix, wo_ref[...])
    out_ref[0] = _layernorm(DN_ALPHA * x_ref[0] + sub, pln_g_ref[...], pln_b_ref[...])


def _const_spec(shape):
    return pl.BlockSpec(shape, lambda b, t: (0,) * len(shape), pipeline_mode=pl.Buffered(1))


def _tile_spec(width):
    return pl.BlockSpec((1, TIME_TILE, width), lambda b, t: (b, t, 0))


def kernel(x, w_in, conf_dw_w, conf_dw_b, conf_ln_g, conf_ln_b, conf_w_out, gdn_conv_w, gdn_A_log,
           gdn_dt_bias, gdn_norm_g, gdn_w_out, w_o, post_ln_g, post_ln_b):
    B, S, D = x.shape
    assert D == D_MODEL and S % TIME_TILE == 0 and TIME_TILE % CHUNK == 0
    H = GDN_HEADS
    o_qkv, o_gz, o_ba, o_gates = 3 * D, 6 * D, 7 * D, 7 * D + 2 * H
    w_conf = w_in[:, :o_qkv].astype(BF16)
    w_qkv = w_in[:, o_qkv:o_gz].astype(BF16)
    w_gz = w_in[:, o_gz:o_ba].astype(BF16)
    w_ba = jnp.pad(w_in[:, o_ba:o_gates], ((0, 0), (0, LANES - 2 * H))).astype(BF16)
    w_gates = w_in[:, o_gates:].astype(BF16)
    dw_w = jnp.pad(conf_dw_w, ((0, CONF_HALO - CONF_KERNEL), (0, 0)))
    gconv_w = jnp.pad(gdn_conv_w, ((0, SUBLANES - GDN_CONV), (0, 0)))
    row = lambda v: v.reshape(1, -1).astype(F32)
    alog = jnp.pad(row(gdn_A_log), ((0, 0), (H, LANES - 2 * H)))
    dtb = jnp.pad(row(gdn_dt_bias), ((0, 0), (H, LANES - 2 * H)))

    weights = (w_conf, w_qkv, w_gz, w_ba, w_gates, dw_w, row(conf_dw_b), row(conf_ln_g),
               row(conf_ln_b), conf_w_out.astype(BF16), gconv_w, alog, dtb, row(gdn_norm_g),
               gdn_w_out.astype(BF16), w_o.astype(BF16), row(post_ln_g), row(post_ln_b))
    ts = TIME_TILE
    return pl.pallas_call(
        _layer_kernel,
        grid=(B, S // ts),
        in_specs=[_tile_spec(D)] + [_const_spec(w.shape) for w in weights],
        out_specs=_tile_spec(D),
        out_shape=jax.ShapeDtypeStruct((B, S, D), x.dtype),
        scratch_shapes=[pltpu.VMEM((ts, D), BF16),
                        pltpu.VMEM((ts + CONF_HALO, D), F32),
                        pltpu.VMEM((ts + GDN_HALO, 3 * D), F32),
                        pltpu.VMEM((ts, D), F32),
                        pltpu.VMEM((ts, 4 * D), F32),
                        pltpu.VMEM((ts, 3 * D), F32),
                        pltpu.VMEM((ts, LANES), F32),
                        pltpu.VMEM((ts, D), BF16),
                        pltpu.VMEM((ts, D), F32),
                        pltpu.VMEM((H, GDN_HEAD_DIM, GDN_HEAD_DIM), F32)],
        compiler_params=pltpu.CompilerParams(dimension_semantics=("arbitrary", "arbitrary"),
                                             vmem_limit_bytes=VMEM_LIMIT_BYTES),
        name="layer",
    )(x, *weights)
```

```python
import jax
import jax.numpy as jnp
from jax import lax
from jax.experimental import pallas as pl
from jax.experimental.pallas import tpu as pltpu

D_MODEL = 1024
CONF_KERNEL = 31
GDN_HEADS = 8
GDN_HEAD_DIM = 128
GDN_CONV = 4
LN_EPS = 1e-5
RMS_EPS = 1e-6
L2_EPS = 1e-6
DN_ALPHA = 2.0 ** 0.25

LANES = 128
SUBLANES = 8
BF16_ROWS = 16
MXU_N = 256
CONF_HALO = 32
GDN_HALO = 8
TIME_TILE = 256
CHUNK = 128
CONF_STEPS_WITH_PROJ = 10
VMEM_LIMIT_BYTES = 60 * 1024 * 1024

BF16 = jnp.bfloat16
F32 = jnp.float32

P_CZ, P_GATE_C, P_GATE_G, P_GZ = (i * D_MODEL for i in range(4))


def _dot(a, b):
    return jnp.dot(a.astype(BF16), b.astype(BF16), preferred_element_type=F32)


def _dot_nt(a, b):
    return lax.dot_general(a.astype(BF16), b.astype(BF16), (((1,), (1,)), ((), ())),
                           preferred_element_type=F32)


def _dot_tn(a, b):
    return lax.dot_general(a.astype(BF16), b.astype(BF16), (((0,), (0,)), ((), ())),
                           preferred_element_type=F32)


def _silu(x):
    return x * jax.nn.sigmoid(x)


def _layernorm(x, g, b):
    mu = jnp.mean(x, axis=-1, keepdims=True)
    xc = x - mu
    var = jnp.mean(xc * xc, axis=-1, keepdims=True)
    return xc * lax.rsqrt(var + LN_EPS) * g + b


def _shift_up(x, r, rows):
    if r == 0:
        return x[:rows]
    return pltpu.roll(x, x.shape[0] - r, axis=0)[:rows]


def _causal_conv(buf_ref, w_ref, n_taps, halo, row0, rows, col0):
    cols = pl.ds(col0, LANES)
    base = halo - (n_taps - 1)
    assert row0 % SUBLANES == 0 and rows % SUBLANES == 0
    acc = None
    for r in range(SUBLANES):
        offs = [o for o in range(base, base + n_taps) if o % SUBLANES == r]
        if not offs:
            continue
        n = rows if r == 0 else rows + SUBLANES
        part = None
        for o in offs:
            term = buf_ref[pl.ds(row0 + o - r, n), cols] * w_ref[pl.ds(o - base, 1), cols]
            part = term if part is None else part + term
        part = _shift_up(part, r, rows)
        acc = part if acc is None else acc + part
    return acc


def _shift_up_packed(x, r, rows):
    if r == 0:
        return x[:rows]
    pairs = pltpu.bitcast(x, jnp.uint32)
    pairs = pltpu.roll(pairs, pairs.shape[0] - r // 2, axis=0)
    return pltpu.bitcast(pairs, BF16)[:rows]


def _conf_conv(even_ref, odd_ref, w_ref, row0, rows, col0):
    cols = pl.ds(col0, LANES)
    base = CONF_HALO - (CONF_KERNEL - 1)
    assert row0 % BF16_ROWS == 0 and rows % BF16_ROWS == 0
    acc = None
    for r in range(0, BF16_ROWS, 2):
        n = rows if r == 0 else rows + BF16_ROWS
        part = None
        for buf_ref, parity in ((even_ref, 0), (odd_ref, 1)):
            for e in range(r, CONF_HALO + 1, BF16_ROWS):
                k = e + parity - base
                if not 0 <= k < CONF_KERNEL:
                    continue
                term = buf_ref[pl.ds(row0 + e - r, n), cols] * w_ref[pl.ds(k, 1), cols].astype(BF16)
                part = term if part is None else part + term
        part = _shift_up_packed(part, r, rows).astype(F32)
        acc = part if acc is None else acc + part
    return acc


def _gdn_conv(buf_ref, w_ref, row0, rows, col0):
    return _causal_conv(buf_ref, w_ref, GDN_CONV, GDN_HALO, row0, rows, col0)


def _interleave(mxu_steps, vpu_steps):
    n_m, n_v = len(mxu_steps), len(vpu_steps)
    done_v = 0
    for i, step in enumerate(mxu_steps):
        step()
        upto = ((i + 1) * n_v) // n_m
        for j in range(done_v, upto):
            vpu_steps[j]()
        done_v = upto
    for j in range(done_v, n_v):
        vpu_steps[j]()


def _delta_rule_chunk(r0, qkvb, bgb, pbuf, state, obuf, norm_g_ref, masks, fill):
    C = CHUNK
    causal, strict, tri, eye = masks
    heads = range(GDN_HEADS)
    rows = pl.ds(r0, C)
    bg = bgb[rows, :]
    gc = jnp.dot(tri, bg, preferred_element_type=F32, precision=lax.Precision.HIGHEST)
    gc_t = gc.T
    cols = [pl.ds(h * GDN_HEAD_DIM, GDN_HEAD_DIM) for h in heads]
    qs = [qkvb[rows, pl.ds(h * GDN_HEAD_DIM, GDN_HEAD_DIM)] for h in heads]
    ks = [qkvb[rows, pl.ds(D_MODEL + h * GDN_HEAD_DIM, GDN_HEAD_DIM)] for h in heads]
    vs = [qkvb[rows, pl.ds(2 * D_MODEL + h * GDN_HEAD_DIM, GDN_HEAD_DIM)] for h in heads]
    betas = [bg[:, h:h + 1] for h in heads]
    g_cols = [gc[:, GDN_HEADS + h:GDN_HEADS + h + 1] for h in heads]
    g_rows = [gc_t[GDN_HEADS + h:GDN_HEADS + h + 1, :] for h in heads]
    e_gs = [jnp.exp(g_cols[h]) for h in heads]
    kbs = [ks[h] * betas[h] for h in heads]
    kqs = [_dot_nt(jnp.concatenate([kbs[h], qs[h]], axis=0), ks[h]) for h in heads]
    fill()
    decays = [jnp.exp(jnp.where(causal, g_cols[h] - g_rows[h], 0.0)) for h in heads]
    ms = [jnp.where(strict, -(kqs[h][:C] * decays[h]), 0.0) for h in heads]
    a_intras = [jnp.where(causal, kqs[h][C:] * decays[h], 0.0) for h in heads]
    invs = [eye + ms[h] for h in heads]
    ps = [_dot(ms[h], ms[h]) for h in heads]
    fill()
    n_pow = 2
    while 2 * n_pow < C:
        rs = [_dot(jnp.concatenate([invs[h], ps[h]], axis=0), ps[h]) for h in heads]
        fill()
        invs = [invs[h] + rs[h][:C] for h in heads]
        ps = [rs[h][C:] for h in heads]
        n_pow *= 2
    invs = [invs[h] + _dot(invs[h], ps[h]) for h in heads]
    fill()
    uws = [_dot(invs[h], jnp.concatenate([vs[h] * betas[h], kbs[h] * e_gs[h]], axis=1))
           for h in heads]
    fill()
    ss = [state[h] for h in heads]
    r2s = [_dot(jnp.concatenate([uws[h][:, GDN_HEAD_DIM:], qs[h] * e_gs[h]], axis=0), ss[h])
           for h in heads]
    fill()
    v_news = [uws[h][:, :GDN_HEAD_DIM] - r2s[h][:C] for h in heads]
    os_ = [r2s[h][C:] + _dot(a_intras[h], v_news[h]) for h in heads]
    fill()
    g_lasts = [g_cols[h][C - 1:C, :] for h in heads]
    k_decs = [ks[h] * jnp.exp(g_lasts[h] - g_cols[h]) for h in heads]
    for h in heads:
        state[h] = ss[h] * jnp.exp(g_lasts[h]) + _dot_tn(k_decs[h], v_news[h])
    fill()
    for h in heads:
        o = os_[h]
        o = o * lax.rsqrt(jnp.mean(o * o, axis=-1, keepdims=True) + RMS_EPS) * norm_g_ref[...]
        gate = pbuf[rows, pl.ds(P_GZ + h * GDN_HEAD_DIM, GDN_HEAD_DIM)]
        obuf[rows, cols[h]] = (o * gate).astype(obuf.dtype)


def _layer_kernel(x_ref, w_conf_ref, w_qkv_ref, w_gz_ref, w_ba_ref, w_gates_ref,
                  dw_w_ref, dw_b_ref, ln_g_ref, ln_b_ref, conf_wo_ref, gconv_w_ref,
                  alog_ref, dtb_ref, norm_g_ref, gdn_wo_ref, wo_ref, pln_g_ref, pln_b_ref,
                  out_ref,
                  xbuf, abuf, a_even, a_odd, cbuf, convb, pbuf, qkvb, bgb, actb, obuf, state):
    ts = x_ref.shape[1]
    t = pl.program_id(1)
    n_tiles = D_MODEL // MXU_N

    @pl.when(t == 0)
    def _():
        abuf[pl.ds(0, CONF_HALO), :] = jnp.zeros((CONF_HALO, D_MODEL), F32)
        cbuf[pl.ds(0, GDN_HALO), :] = jnp.zeros((GDN_HALO, 3 * D_MODEL), F32)
        state[...] = jnp.zeros(state.shape, F32)

    xbuf[...] = x_ref[0].astype(BF16)

    def proj(w_ref, wcol):
        return jnp.dot(xbuf[...], w_ref[:, pl.ds(wcol, MXU_N)], preferred_element_type=F32)

    for j in range(n_tiles):
        c_val = proj(w_conf_ref, j * MXU_N)
        c_glu = proj(w_conf_ref, D_MODEL + j * MXU_N)
        abuf[pl.ds(CONF_HALO, ts), pl.ds(j * MXU_N, MXU_N)] = c_val * jax.nn.sigmoid(c_glu)
        for c in range(j * MXU_N // LANES, (j + 1) * MXU_N // LANES):
            cols = pl.ds(c * LANES, LANES)
            a = abuf[:, cols]
            a_even[:, cols] = a.astype(BF16)
            a_odd[:, cols] = pltpu.roll(a, a.shape[0] - 1, axis=0).astype(BF16)

    def qkv_step(j):
        def step():
            cbuf[pl.ds(GDN_HALO, ts), pl.ds(j * MXU_N, MXU_N)] = proj(w_qkv_ref, j * MXU_N)
        return step

    def gate_step(w_ref, wcol, pcol, activation):
        def step():
            pbuf[:, pl.ds(pcol, MXU_N)] = activation(proj(w_ref, wcol))
        return step

    def conf_conv_step(row0, rows, c):
        def step():
            cols = pl.ds(c * LANES, LANES)
            convb[pl.ds(row0, rows), cols] = (_conf_conv(a_even, a_odd, dw_w_ref, row0, rows, c * LANES)
                                              + dw_b_ref[:, cols])
        return step

    def gdn_conv_step(part, h):
        def step():
            col0 = part * D_MODEL + h * GDN_HEAD_DIM
            y = _silu(_gdn_conv(cbuf, gconv_w_ref, 0, ts, col0))
            if part < 2:
                y = y * lax.rsqrt(jnp.sum(y * y, axis=-1, keepdims=True) + L2_EPS)
                if part == 0:
                    y = y * (GDN_HEAD_DIM ** -0.5)
            qkvb[:, pl.ds(col0, GDN_HEAD_DIM)] = y
        return step

    def conf_act_step(row0, rows):
        def step():
            r = pl.ds(row0, rows)
            act = _silu(_layernorm(convb[r, :], ln_g_ref[...], ln_b_ref[...]))
            actb[r, :] = (act * pbuf[r, pl.ds(P_CZ, D_MODEL)]).astype(BF16)
        return step

    def conf_out_step(j):
        def step():
            cols = pl.ds(P_GATE_C + j * MXU_N, MXU_N)
            y_conf = jnp.dot(actb[...], conf_wo_ref[:, pl.ds(j * MXU_N, MXU_N)],
                             preferred_element_type=F32)
            pbuf[:, cols] = pbuf[:, cols] * y_conf
        return step

    gate_steps = []
    for j in range(n_tiles):
        gate_steps.append(gate_step(w_conf_ref, 2 * D_MODEL + j * MXU_N, P_CZ + j * MXU_N, _silu))
    for j in range(n_tiles):
        gate_steps.append(gate_step(w_gates_ref, j * MXU_N, P_GATE_C + j * MXU_N, jax.nn.sigmoid))
    for j in range(n_tiles):
        gate_steps.append(gate_step(w_gates_ref, D_MODEL + j * MXU_N, P_GATE_G + j * MXU_N,
                                    jax.nn.sigmoid))
    for j in range(n_tiles):
        gate_steps.append(gate_step(w_gz_ref, j * MXU_N, P_GZ + j * MXU_N, _silu))
    qkv_steps = [qkv_step(j) for j in range(3 * n_tiles)]
    half = ts // 2
    conf_steps = [conf_conv_step(r0, half, c) for r0 in (0, half) for c in range(D_MODEL // LANES)]
    gdn_steps = [gdn_conv_step(part, h) for part in range(3) for h in range(GDN_HEADS)]
    n_early = CONF_STEPS_WITH_PROJ
    _interleave(qkv_steps + gate_steps, conf_steps[:n_early] + gdn_steps)

    ba = jnp.dot(xbuf[...], w_ba_ref[...], preferred_element_type=F32)
    z = ba + dtb_ref[...]
    softplus = jnp.maximum(z, 0.0) + jnp.log1p(jnp.exp(-jnp.abs(z)))
    g = -jnp.exp(alog_ref[...]) * softplus
    lane = lax.broadcasted_iota(jnp.int32, ba.shape, 1)
    bgb[...] = jnp.where(lane < GDN_HEADS, jax.nn.sigmoid(ba), g)

    pending = (conf_steps[n_early:] + [conf_act_step(r0, ts // 4) for r0 in range(0, ts, ts // 4)]
               + [conf_out_step(j) for j in range(n_tiles)])

    def fill():
        if pending:
            pending.pop(0)()

    C = CHUNK
    row = lax.broadcasted_iota(jnp.int32, (C, C), 0)
    col = lax.broadcasted_iota(jnp.int32, (C, C), 1)
    causal = row >= col
    masks = (causal, row > col, causal.astype(F32), (row == col).astype(F32))
    for c in range(ts // C):
        _delta_rule_chunk(c * C, qkvb, bgb, pbuf, state, obuf, norm_g_ref, masks, fill)
    while pending:
        fill()
    abuf[pl.ds(0, CONF_HALO), :] = abuf[pl.ds(ts, CONF_HALO), :]
    cbuf[pl.ds(0, GDN_HALO), :] = cbuf[pl.ds(ts, GDN_HALO), :]

    for j in range(n_tiles):
        cols = pl.ds(j * MXU_N, MXU_N)
        y_gdn = jnp.dot(obuf[...], gdn_wo_ref[:, cols], preferred_element_type=F32)
        hmix = (pbuf[:, pl.ds(P_GATE_C + j * MXU_N, MXU_N)]
                + pbuf[:, pl.ds(P_GATE_G + j * MXU_N, MXU_N)] * y_gdn)
        actb[:, cols] = hmix.astype(BF16)
    for j in range(n_tiles):
        cols = pl.ds(j * MXU_N, MXU_N)
        sub = jnp.dot(actb[...], wo_ref[:, cols], preferred_element_type=F32)
        convb[:, cols] = DN_ALPHA * x_ref[0, :, cols] + sub
    out_ref[0] = _layernorm(convb[...], pln_g_ref[...], pln_b_ref[...])


def _const_spec(shape):
    return pl.BlockSpec(shape, lambda b, t: (0,) * len(shape), pipeline_mode=pl.Buffered(1))


def _tile_spec(width):
    return pl.BlockSpec((1, TIME_TILE, width), lambda b, t: (b, t, 0))


def kernel(x, w_in, conf_dw_w, conf_dw_b, conf_ln_g, conf_ln_b, conf_w_out, gdn_conv_w, gdn_A_log,
           gdn_dt_bias, gdn_norm_g, gdn_w_out, w_o, post_ln_g, post_ln_b):
    B, S, D = x.shape
    assert D == D_MODEL and S % TIME_TILE == 0 and TIME_TILE % CHUNK == 0
    H = GDN_HEADS
    o_qkv, o_gz, o_ba, o_gates = 3 * D, 6 * D, 7 * D, 7 * D + 2 * H
    w_bf = w_in.astype(BF16)
    w_conf = w_bf[:, :o_qkv]
    w_qkv = w_bf[:, o_qkv:o_gz]
    w_gz = w_bf[:, o_gz:o_ba]
    w_ba = jnp.pad(w_bf[:, o_ba:o_gates], ((0, 0), (0, LANES - 2 * H)))
    w_gates = w_bf[:, o_gates:]
    dw_w = jnp.pad(conf_dw_w, ((0, CONF_HALO - CONF_KERNEL), (0, 0)))
    gconv_w = jnp.pad(gdn_conv_w, ((0, SUBLANES - GDN_CONV), (0, 0)))
    row = lambda v: v.reshape(1, -1).astype(F32)
    alog = jnp.pad(row(gdn_A_log), ((0, 0), (H, LANES - 2 * H)))
    dtb = jnp.pad(row(gdn_dt_bias), ((0, 0), (H, LANES - 2 * H)))

    weights = (w_conf, w_qkv, w_gz, w_ba, w_gates, dw_w, row(conf_dw_b), row(conf_ln_g),
               row(conf_ln_b), conf_w_out.astype(BF16), gconv_w, alog, dtb, row(gdn_norm_g),
               gdn_w_out.astype(BF16), w_o.astype(BF16), row(post_ln_g), row(post_ln_b))
    ts = TIME_TILE
    return pl.pallas_call(
        _layer_kernel,
        grid=(B, S // ts),
        in_specs=[_tile_spec(D)] + [_const_spec(w.shape) for w in weights],
        out_specs=_tile_spec(D),
        out_shape=jax.ShapeDtypeStruct((B, S, D), x.dtype),
        scratch_shapes=[pltpu.VMEM((ts, D), BF16),
                        pltpu.VMEM((ts + CONF_HALO, D), F32),
                        pltpu.VMEM((ts + CONF_HALO, D), BF16),
                        pltpu.VMEM((ts + CONF_HALO, D), BF16),
                        pltpu.VMEM((ts + GDN_HALO, 3 * D), F32),
                        pltpu.VMEM((ts, D), F32),
                        pltpu.VMEM((ts, 4 * D), F32),
                        pltpu.VMEM((ts, 3 * D), F32),
                        pltpu.VMEM((ts, LANES), F32),
                        pltpu.VMEM((ts, D), BF16),
                        pltpu.VMEM((ts, D), BF16),
                        pltpu.VMEM((H, GDN_HEAD_DIM, GDN_HEAD_DIM), F32)],
        compiler_params=pltpu.CompilerParams(dimension_semantics=("arbitrary", "arbitrary"),
                                             vmem_limit_bytes=VMEM_LIMIT_BYTES),
        name="layer",
    )(x, *weights)
```

```python
import jax
import jax.numpy as jnp
from jax import lax
from jax.experimental import pallas as pl
from jax.experimental.pallas import tpu as pltpu

D_MODEL = 1024
CONF_KERNEL = 31
GDN_HEADS = 8
GDN_HEAD_DIM = 128
GDN_CONV = 4
LN_EPS = 1e-5
RMS_EPS = 1e-6
L2_EPS = 1e-6
DN_ALPHA = 2.0 ** 0.25

LANES = 128
SUBLANES = 8
MXU_N = 256
CONF_HALO = 32
GDN_HALO = 8
TIME_TILE = 256
CHUNK = 128
CONF_STEPS_WITH_PROJ = 10
VMEM_LIMIT_BYTES = 60 * 1024 * 1024

BF16 = jnp.bfloat16
F32 = jnp.float32

P_CZ, P_GATE_C, P_GATE_G, P_GZ = (i * D_MODEL for i in range(4))


def _dot(a, b):
    return jnp.dot(a.astype(BF16), b.astype(BF16), preferred_element_type=F32)


def _dot_nt(a, b):
    return lax.dot_general(a.astype(BF16), b.astype(BF16), (((1,), (1,)), ((), ())),
                           preferred_element_type=F32)


def _dot_tn(a, b):
    return lax.dot_general(a.astype(BF16), b.astype(BF16), (((0,), (0,)), ((), ())),
                           preferred_element_type=F32)


def _silu(x):
    return x * jax.nn.sigmoid(x)


def _layernorm(x, g, b):
    mu = jnp.mean(x, axis=-1, keepdims=True)
    xc = x - mu
    var = jnp.mean(xc * xc, axis=-1, keepdims=True)
    return xc * lax.rsqrt(var + LN_EPS) * g + b


def _shift_up(x, r, rows):
    if r == 0:
        return x[:rows]
    return pltpu.roll(x, x.shape[0] - r, axis=0)[:rows]


def _conf_conv(buf_ref, w_ref, row0, rows, col0):
    cols = pl.ds(col0, LANES)
    base = CONF_HALO - (CONF_KERNEL - 1)
    assert row0 % SUBLANES == 0 and rows % SUBLANES == 0
    acc = None
    for r in range(SUBLANES):
        offs = [o for o in range(base, base + CONF_KERNEL) if o % SUBLANES == r]
        if not offs:
            continue
        n = rows if r == 0 else rows + SUBLANES
        part = None
        for o in offs:
            term = buf_ref[pl.ds(row0 + o - r, n), cols] * w_ref[pl.ds(o - base, 1), cols]
            part = term if part is None else part + term
        part = _shift_up(part, r, rows)
        acc = part if acc is None else acc + part
    return acc


def _gdn_conv(buf_ref, w_ref, row0, rows, col0):
    assert GDN_CONV == 4 and GDN_HALO == SUBLANES and row0 % SUBLANES == 0
    cols = pl.ds(col0, LANES)
    w = [w_ref[pl.ds(k, 1), cols] for k in range(GDN_CONV)]
    c_lo = buf_ref[pl.ds(row0, rows + SUBLANES), cols]
    d_lo = pltpu.roll(c_lo, 1, axis=0)
    c, d = c_lo[SUBLANES:], d_lo[SUBLANES:]
    pair = w[1] * c_lo + w[0] * d_lo
    return w[3] * c + w[2] * d + _shift_up(pair, SUBLANES - 2, rows)


def _interleave(mxu_steps, vpu_steps):
    n_m, n_v = len(mxu_steps), len(vpu_steps)
    done_v = 0
    for i, step in enumerate(mxu_steps):
        step()
        upto = ((i + 1) * n_v) // n_m
        for j in range(done_v, upto):
            vpu_steps[j]()
        done_v = upto
    for j in range(done_v, n_v):
        vpu_steps[j]()


def _delta_rule_chunk(r0, qkvb, bgb, pbuf, state, obuf, norm_g_ref, masks, fill):
    C = CHUNK
    causal, strict, tri, eye = masks
    heads = range(GDN_HEADS)
    rows = pl.ds(r0, C)
    bg = bgb[rows, :]
    gc = jnp.dot(tri, bg, preferred_element_type=F32, precision=lax.Precision.HIGHEST)
    gc_t = gc.T
    cols = [pl.ds(h * GDN_HEAD_DIM, GDN_HEAD_DIM) for h in heads]
    qs = [qkvb[rows, pl.ds(h * GDN_HEAD_DIM, GDN_HEAD_DIM)] for h in heads]
    ks = [qkvb[rows, pl.ds(D_MODEL + h * GDN_HEAD_DIM, GDN_HEAD_DIM)] for h in heads]
    vs = [qkvb[rows, pl.ds(2 * D_MODEL + h * GDN_HEAD_DIM, GDN_HEAD_DIM)] for h in heads]
    betas = [bg[:, h:h + 1] for h in heads]
    g_cols = [gc[:, GDN_HEADS + h:GDN_HEADS + h + 1] for h in heads]
    g_rows = [gc_t[GDN_HEADS + h:GDN_HEADS + h + 1, :] for h in heads]
    e_gs = [jnp.exp(g_cols[h]) for h in heads]
    kbs = [ks[h] * betas[h] for h in heads]
    kqs = [_dot_nt(jnp.concatenate([kbs[h], qs[h]], axis=0), ks[h]) for h in heads]
    fill()
    decays = [jnp.exp(jnp.where(causal, g_cols[h] - g_rows[h], 0.0)) for h in heads]
    ls = [jnp.where(strict, kqs[h][:C] * decays[h], 0.0) for h in heads]
    a_intras = [jnp.where(causal, kqs[h][C:] * decays[h], 0.0) for h in heads]
    invs = [eye - ls[h] for h in heads]
    ps = [_dot(ls[h], ls[h]) for h in heads]
    fill()
    n_pow = 2
    while 2 * n_pow < C:
        rs = [_dot(jnp.concatenate([invs[h], ps[h]], axis=0), ps[h]) for h in heads]
        fill()
        invs = [invs[h] + rs[h][:C] for h in heads]
        ps = [rs[h][C:] for h in heads]
        n_pow *= 2
    invs = [invs[h] + _dot(invs[h], ps[h]) for h in heads]
    fill()
    uws = [_dot(invs[h], jnp.concatenate([vs[h] * betas[h], kbs[h] * e_gs[h]], axis=1))
           for h in heads]
    fill()
    ss = [state[h] for h in heads]
    r2s = [_dot(jnp.concatenate([uws[h][:, GDN_HEAD_DIM:], qs[h] * e_gs[h]], axis=0), ss[h])
           for h in heads]
    fill()
    v_news = [uws[h][:, :GDN_HEAD_DIM] - r2s[h][:C] for h in heads]
    os_ = [r2s[h][C:] + _dot(a_intras[h], v_news[h]) for h in heads]
    fill()
    g_lasts = [g_cols[h][C - 1:C, :] for h in heads]
    k_decs = [ks[h] * jnp.exp(g_lasts[h] - g_cols[h]) for h in heads]
    for h in heads:
        state[h] = ss[h] * jnp.exp(g_lasts[h]) + _dot_tn(k_decs[h], v_news[h])
    fill()
    for h in heads:
        o = os_[h]
        o = o * lax.rsqrt(jnp.mean(o * o, axis=-1, keepdims=True) + RMS_EPS) * norm_g_ref[...]
        obuf[rows, cols[h]] = o * _silu(pbuf[rows, pl.ds(P_GZ + h * GDN_HEAD_DIM, GDN_HEAD_DIM)])


def _layer_kernel(x_ref, w_conf_ref, w_qkv_ref, w_gz_ref, w_ba_ref, w_gates_ref,
                  dw_w_ref, dw_b_ref, ln_g_ref, ln_b_ref, conf_wo_ref, gconv_w_ref,
                  alog_ref, dtb_ref, norm_g_ref, gdn_wo_ref, wo_ref, pln_g_ref, pln_b_ref,
                  out_ref,
                  xbuf, abuf, cbuf, convb, pbuf, qkvb, bgb, actb, obuf, state):
    ts = x_ref.shape[1]
    t = pl.program_id(1)
    n_tiles = D_MODEL // MXU_N

    @pl.when(t == 0)
    def _():
        abuf[pl.ds(0, CONF_HALO), :] = jnp.zeros((CONF_HALO, D_MODEL), F32)
        cbuf[pl.ds(0, GDN_HALO), :] = jnp.zeros((GDN_HALO, 3 * D_MODEL), F32)
        state[...] = jnp.zeros(state.shape, F32)

    xbuf[...] = x_ref[0].astype(BF16)

    def proj(w_ref, wcol):
        return jnp.dot(xbuf[...], w_ref[:, pl.ds(wcol, MXU_N)], preferred_element_type=F32)

    for j in range(n_tiles):
        c_val = proj(w_conf_ref, j * MXU_N)
        c_glu = proj(w_conf_ref, D_MODEL + j * MXU_N)
        abuf[pl.ds(CONF_HALO, ts), pl.ds(j * MXU_N, MXU_N)] = c_val * jax.nn.sigmoid(c_glu)

    def qkv_step(j):
        def step():
            cbuf[pl.ds(GDN_HALO, ts), pl.ds(j * MXU_N, MXU_N)] = proj(w_qkv_ref, j * MXU_N)
        return step

    def raw_step(w_ref, wcol, pcol):
        def step():
            pbuf[:, pl.ds(pcol, MXU_N)] = proj(w_ref, wcol)
        return step

    def conf_conv_step(row0, rows, c):
        def step():
            cols = pl.ds(c * LANES, LANES)
            convb[pl.ds(row0, rows), cols] = (_conf_conv(abuf, dw_w_ref, row0, rows, c * LANES)
                                              + dw_b_ref[:, cols])
        return step

    def gdn_conv_step(part, h):
        def step():
            col0 = part * D_MODEL + h * GDN_HEAD_DIM
            y = _silu(_gdn_conv(cbuf, gconv_w_ref, 0, ts, col0))
            if part < 2:
                y = y * lax.rsqrt(jnp.sum(y * y, axis=-1, keepdims=True) + L2_EPS)
                if part == 0:
                    y = y * (GDN_HEAD_DIM ** -0.5)
            qkvb[:, pl.ds(col0, GDN_HEAD_DIM)] = y
        return step

    def conf_act_step(row0, rows):
        def step():
            r = pl.ds(row0, rows)
            act = _silu(_layernorm(convb[r, :], ln_g_ref[...], ln_b_ref[...]))
            actb[r, :] = (act * _silu(pbuf[r, pl.ds(P_CZ, D_MODEL)])).astype(BF16)
        return step

    def conf_out_step(j):
        def step():
            cols = pl.ds(P_GATE_C + j * MXU_N, MXU_N)
            y_conf = jnp.dot(actb[...], conf_wo_ref[:, pl.ds(j * MXU_N, MXU_N)],
                             preferred_element_type=F32)
            pbuf[:, cols] = jax.nn.sigmoid(pbuf[:, cols]) * y_conf
        return step

    raw_steps = []
    for j in range(n_tiles):
        raw_steps.append(raw_step(w_conf_ref, 2 * D_MODEL + j * MXU_N, P_CZ + j * MXU_N))
    for j in range(n_tiles):
        raw_steps.append(raw_step(w_gates_ref, j * MXU_N, P_GATE_C + j * MXU_N))
    for j in range(n_tiles):
        raw_steps.append(raw_step(w_gates_ref, D_MODEL + j * MXU_N, P_GATE_G + j * MXU_N))
    for j in range(n_tiles):
        raw_steps.append(raw_step(w_gz_ref, j * MXU_N, P_GZ + j * MXU_N))
    qkv_steps = [qkv_step(j) for j in range(3 * n_tiles)]
    half = ts // 2
    conf_steps = [conf_conv_step(r0, half, c) for r0 in (0, half) for c in range(D_MODEL // LANES)]
    gdn_steps = [gdn_conv_step(part, h) for part in range(3) for h in range(GDN_HEADS)]
    n_early = CONF_STEPS_WITH_PROJ
    _interleave(qkv_steps + raw_steps, conf_steps[:n_early] + gdn_steps)

    ba = jnp.dot(xbuf[...], w_ba_ref[...], preferred_element_type=F32)
    z = ba + dtb_ref[...]
    softplus = jnp.maximum(z, 0.0) + jnp.log1p(jnp.exp(-jnp.abs(z)))
    g = -jnp.exp(alog_ref[...]) * softplus
    lane = lax.broadcasted_iota(jnp.int32, ba.shape, 1)
    bgb[...] = jnp.where(lane < GDN_HEADS, jax.nn.sigmoid(ba), g)

    pending = (conf_steps[n_early:] + [conf_act_step(r0, ts // 4) for r0 in range(0, ts, ts // 4)]
               + [conf_out_step(j) for j in range(n_tiles)])

    def fill():
        if pending:
            pending.pop(0)()

    C = CHUNK
    row = lax.broadcasted_iota(jnp.int32, (C, C), 0)
    col = lax.broadcasted_iota(jnp.int32, (C, C), 1)
    causal = row >= col
    masks = (causal, row > col, causal.astype(F32), (row == col).astype(F32))
    for c in range(ts // C):
        _delta_rule_chunk(c * C, qkvb, bgb, pbuf, state, obuf, norm_g_ref, masks, fill)
    while pending:
        fill()
    abuf[pl.ds(0, CONF_HALO), :] = abuf[pl.ds(ts, CONF_HALO), :]
    cbuf[pl.ds(0, GDN_HALO), :] = cbuf[pl.ds(ts, GDN_HALO), :]

    y_gdn = _dot(obuf[...], gdn_wo_ref[...])
    hmix = pbuf[:, pl.ds(P_GATE_C, D_MODEL)] + jax.nn.sigmoid(pbuf[:, pl.ds(P_GATE_G, D_MODEL)]) * y_gdn
    sub = _dot(hmix, wo_ref[...])
    out_ref[0] = _layernorm(DN_ALPHA * x_ref[0] + sub, pln_g_ref[...], pln_b_ref[...])


def _const_spec(shape):
    return pl.BlockSpec(shape, lambda b, t: (0,) * len(shape), pipeline_mode=pl.Buffered(1))


def _tile_spec(width):
    return pl.BlockSpec((1, TIME_TILE, width), lambda b, t: (b, t, 0))


def kernel(x, w_in, conf_dw_w, conf_dw_b, conf_ln_g, conf_ln_b, conf_w_out, gdn_conv_w, gdn_A_log,
           gdn_dt_bias, gdn_norm_g, gdn_w_out, w_o, post_ln_g, post_ln_b):
    B, S, D = x.shape
    assert D == D_MODEL and S % TIME_TILE == 0 and TIME_TILE % CHUNK == 0
    H = GDN_HEADS
    o_qkv, o_gz, o_ba, o_gates = 3 * D, 6 * D, 7 * D, 7 * D + 2 * H
    w_conf = w_in[:, :o_qkv].astype(BF16)
    w_qkv = w_in[:, o_qkv:o_gz].astype(BF16)
    w_gz = w_in[:, o_gz:o_ba].astype(BF16)
    w_ba = jnp.pad(w_in[:, o_ba:o_gates], ((0, 0), (0, LANES - 2 * H))).astype(BF16)
    w_gates = w_in[:, o_gates:].astype(BF16)
    dw_w = jnp.pad(conf_dw_w, ((0, CONF_HALO - CONF_KERNEL), (0, 0)))
    gconv_w = jnp.pad(gdn_conv_w, ((0, SUBLANES - GDN_CONV), (0, 0)))
    row = lambda v: v.reshape(1, -1).astype(F32)
    alog = jnp.pad(row(gdn_A_log), ((0, 0), (H, LANES - 2 * H)))
    dtb = jnp.pad(row(gdn_dt_bias), ((0, 0), (H, LANES - 2 * H)))

    weights = (w_conf, w_qkv, w_gz, w_ba, w_gates, dw_w, row(conf_dw_b), row(conf_ln_g),
               row(conf_ln_b), conf_w_out.astype(BF16), gconv_w, alog, dtb, row(gdn_norm_g),
               gdn_w_out.astype(BF16), w_o.astype(BF16), row(post_ln_g), row(post_ln_b))
    ts = TIME_TILE
    return pl.pallas_call(
        _layer_kernel,
        grid=(B, S // ts),
        in_specs=[_tile_spec(D)] + [_const_spec(w.shape) for w in weights],
        out_specs=_tile_spec(D),
        out_shape=jax.ShapeDtypeStruct((B, S, D), x.dtype),
        scratch_shapes=[pltpu.VMEM((ts, D), BF16),
                        pltpu.VMEM((ts + CONF_HALO, D), F32),
                        pltpu.VMEM((ts + GDN_HALO, 3 * D), F32),
                        pltpu.VMEM((ts, D), F32),
                        pltpu.VMEM((ts, 4 * D), F32),
                        pltpu.VMEM((ts, 3 * D), F32),
                        pltpu.VMEM((ts, LANES), F32),
                        pltpu.VMEM((ts, D), BF16),
                        pltpu.VMEM((ts, D), F32),
                        pltpu.VMEM((H, GDN_HEAD_DIM, GDN_HEAD_DIM), F32)],
        compiler_params=pltpu.CompilerParams(dimension_semantics=("arbitrary", "arbitrary"),
                                             vmem_limit_bytes=VMEM_LIMIT_BYTES),
        name="layer",
    )(x, *weights)
```

```python
import jax
import jax.numpy as jnp
from jax import lax
from jax.experimental import pallas as pl
from jax.experimental.pallas import tpu as pltpu

D_MODEL = 1024
CONF_KERNEL = 31
GDN_HEADS = 8
GDN_HEAD_DIM = 128
GDN_CONV = 4
LN_EPS = 1e-5
RMS_EPS = 1e-6
L2_EPS = 1e-6
DN_ALPHA = 2.0 ** 0.25

LANES = 128
SUBLANES = 8
MXU_N = 256
CONF_HALO = 32
GDN_HALO = 8
TIME_TILE = 256
CHUNK = 128
VMEM_LIMIT_BYTES = 60 * 1024 * 1024

BF16 = jnp.bfloat16
F32 = jnp.float32

P_CZ, P_GATE_C, P_GATE_G, P_GZ = (i * D_MODEL for i in range(4))


def _dot(a, b):
    return jnp.dot(a.astype(BF16), b.astype(BF16), preferred_element_type=F32)


def _dot_nt(a, b):
    return lax.dot_general(a.astype(BF16), b.astype(BF16), (((1,), (1,)), ((), ())),
                           preferred_element_type=F32)


def _dot_tn(a, b):
    return lax.dot_general(a.astype(BF16), b.astype(BF16), (((0,), (0,)), ((), ())),
                           preferred_element_type=F32)


def _silu(x):
    return x * jax.nn.sigmoid(x)


def _layernorm(x, g, b):
    mu = jnp.mean(x, axis=-1, keepdims=True)
    xc = x - mu
    var = jnp.mean(xc * xc, axis=-1, keepdims=True)
    return xc * lax.rsqrt(var + LN_EPS) * g + b


def _shift_up(x, r, rows):
    if r == 0:
        return x[:rows]
    return pltpu.roll(x, x.shape[0] - r, axis=0)[:rows]


def _conf_conv(buf_ref, w_ref, row0, rows, col0):
    cols = pl.ds(col0, LANES)
    base = CONF_HALO - (CONF_KERNEL - 1)
    assert row0 % SUBLANES == 0 and rows % SUBLANES == 0
    acc = None
    for r in range(SUBLANES):
        offs = [o for o in range(base, base + CONF_KERNEL) if o % SUBLANES == r]
        if not offs:
            continue
        n = rows if r == 0 else rows + SUBLANES
        part = None
        for o in offs:
            term = buf_ref[pl.ds(row0 + o - r, n), cols] * w_ref[pl.ds(o - base, 1), cols]
            part = term if part is None else part + term
        part = _shift_up(part, r, rows)
        acc = part if acc is None else acc + part
    return acc


def _gdn_conv(buf_ref, w_ref, row0, rows, col0):
    assert GDN_CONV == 4 and GDN_HALO == SUBLANES and row0 % SUBLANES == 0
    cols = pl.ds(col0, LANES)
    w = [w_ref[pl.ds(k, 1), cols] for k in range(GDN_CONV)]
    c_lo = buf_ref[pl.ds(row0, rows + SUBLANES), cols]
    d_lo = pltpu.roll(c_lo, 1, axis=0)
    c, d = c_lo[SUBLANES:], d_lo[SUBLANES:]
    pair = w[1] * c_lo + w[0] * d_lo
    return w[3] * c + w[2] * d + _shift_up(pair, SUBLANES - 2, rows)


def _interleave(mxu_steps, vpu_steps):
    n_m, n_v = len(mxu_steps), len(vpu_steps)
    done_v = 0
    for i, step in enumerate(mxu_steps):
        step()
        upto = ((i + 1) * n_v) // n_m
        for j in range(done_v, upto):
            vpu_steps[j]()
        done_v = upto
    for j in range(done_v, n_v):
        vpu_steps[j]()


def _delta_rule_chunk(r0, qkvb, bgb, pbuf, state, obuf, norm_g_ref, masks, fill):
    C = CHUNK
    causal, strict, tri, eye = masks
    heads = range(GDN_HEADS)
    rows = pl.ds(r0, C)
    bg = bgb[rows, :]
    gc = jnp.dot(tri, bg, preferred_element_type=F32, precision=lax.Precision.HIGHEST)
    gc_t = gc.T
    cols = [pl.ds(h * GDN_HEAD_DIM, GDN_HEAD_DIM) for h in heads]
    qs = [qkvb[rows, pl.ds(h * GDN_HEAD_DIM, GDN_HEAD_DIM)] for h in heads]
    ks = [qkvb[rows, pl.ds(D_MODEL + h * GDN_HEAD_DIM, GDN_HEAD_DIM)] for h in heads]
    vs = [qkvb[rows, pl.ds(2 * D_MODEL + h * GDN_HEAD_DIM, GDN_HEAD_DIM)] for h in heads]
    betas = [bg[:, h:h + 1] for h in heads]
    g_cols = [gc[:, GDN_HEADS + h:GDN_HEADS + h + 1] for h in heads]
    g_rows = [gc_t[GDN_HEADS + h:GDN_HEADS + h + 1, :] for h in heads]
    e_gs = [jnp.exp(g_cols[h]) for h in heads]
    kbs = [ks[h] * betas[h] for h in heads]
    kqs = [_dot_nt(jnp.concatenate([kbs[h], qs[h]], axis=0), ks[h]) for h in heads]
    fill()
    decays = [jnp.exp(jnp.where(causal, g_cols[h] - g_rows[h], 0.0)) for h in heads]
    ls = [jnp.where(strict, kqs[h][:C] * decays[h], 0.0) for h in heads]
    a_intras = [jnp.where(causal, kqs[h][C:] * decays[h], 0.0) for h in heads]
    invs = [eye - ls[h] for h in heads]
    ps = [_dot(ls[h], ls[h]) for h in heads]
    fill()
    n_pow = 2
    while 2 * n_pow < C:
        rs = [_dot(jnp.concatenate([invs[h], ps[h]], axis=0), ps[h]) for h in heads]
        fill()
        invs = [invs[h] + rs[h][:C] for h in heads]
        ps = [rs[h][C:] for h in heads]
        n_pow *= 2
    invs = [invs[h] + _dot(invs[h], ps[h]) for h in heads]
    fill()
    uws = [_dot(invs[h], jnp.concatenate([vs[h] * betas[h], kbs[h] * e_gs[h]], axis=1))
           for h in heads]
    fill()
    ss = [state[h] for h in heads]
    r2s = [_dot(jnp.concatenate([uws[h][:, GDN_HEAD_DIM:], qs[h] * e_gs[h]], axis=0), ss[h])
           for h in heads]
    fill()
    v_news = [uws[h][:, :GDN_HEAD_DIM] - r2s[h][:C] for h in heads]
    os_ = [r2s[h][C:] + _dot(a_intras[h], v_news[h]) for h in heads]
    fill()
    g_lasts = [g_cols[h][C - 1:C, :] for h in heads]
    k_decs = [ks[h] * jnp.exp(g_lasts[h] - g_cols[h]) for h in heads]
    for h in heads:
        state[h] = ss[h] * jnp.exp(g_lasts[h]) + _dot_tn(k_decs[h], v_news[h])
    fill()
    for h in heads:
        o = os_[h]
        o = o * lax.rsqrt(jnp.mean(o * o, axis=-1, keepdims=True) + RMS_EPS) * norm_g_ref[...]
        obuf[rows, cols[h]] = o * _silu(pbuf[rows, pl.ds(P_GZ + h * GDN_HEAD_DIM, GDN_HEAD_DIM)])


def _layer_kernel(x_ref, w_conf_ref, w_qkv_ref, w_gz_ref, w_ba_ref, w_gates_ref,
                  dw_w_ref, dw_b_ref, ln_g_ref, ln_b_ref, conf_wo_ref, gconv_w_ref,
                  alog_ref, dtb_ref, norm_g_ref, gdn_wo_ref, wo_ref, pln_g_ref, pln_b_ref,
                  out_ref,
                  xbuf, abuf, cbuf, convb, pbuf, qkvb, bgb, actb, obuf, state):
    ts = x_ref.shape[1]
    t = pl.program_id(1)
    n_tiles = D_MODEL // MXU_N

    @pl.when(t == 0)
    def _():
        abuf[pl.ds(0, CONF_HALO), :] = jnp.zeros((CONF_HALO, D_MODEL), F32)
        cbuf[pl.ds(0, GDN_HALO), :] = jnp.zeros((GDN_HALO, 3 * D_MODEL), F32)
        state[...] = jnp.zeros(state.shape, F32)

    xbuf[...] = x_ref[0].astype(BF16)

    def proj(w_ref, wcol):
        return jnp.dot(xbuf[...], w_ref[:, pl.ds(wcol, MXU_N)], preferred_element_type=F32)

    for j in range(n_tiles):
        c_val = proj(w_conf_ref, j * MXU_N)
        c_glu = proj(w_conf_ref, D_MODEL + j * MXU_N)
        abuf[pl.ds(CONF_HALO, ts), pl.ds(j * MXU_N, MXU_N)] = c_val * jax.nn.sigmoid(c_glu)

    def qkv_step(j):
        def step():
            cbuf[pl.ds(GDN_HALO, ts), pl.ds(j * MXU_N, MXU_N)] = proj(w_qkv_ref, j * MXU_N)
        return step

    def raw_step(w_ref, wcol, pcol):
        def step():
            pbuf[:, pl.ds(pcol, MXU_N)] = proj(w_ref, wcol)
        return step

    def conf_conv_step(row0, rows, c):
        def step():
            cols = pl.ds(c * LANES, LANES)
            convb[pl.ds(row0, rows), cols] = (_conf_conv(abuf, dw_w_ref, row0, rows, c * LANES)
                                              + dw_b_ref[:, cols])
        return step

    def gdn_conv_step(part, h):
        def step():
            col0 = part * D_MODEL + h * GDN_HEAD_DIM
            y = _silu(_gdn_conv(cbuf, gconv_w_ref, 0, ts, col0))
            if part < 2:
                y = y * lax.rsqrt(jnp.sum(y * y, axis=-1, keepdims=True) + L2_EPS)
                if part == 0:
                    y = y * (GDN_HEAD_DIM ** -0.5)
            qkvb[:, pl.ds(col0, GDN_HEAD_DIM)] = y
        return step

    def conf_act_step(row0, rows):
        def step():
            r = pl.ds(row0, rows)
            act = _silu(_layernorm(convb[r, :], ln_g_ref[...], ln_b_ref[...]))
            actb[r, :] = (act * _silu(pbuf[r, pl.ds(P_CZ, D_MODEL)])).astype(BF16)
        return step

    def conf_out_step(j):
        def step():
            cols = pl.ds(P_GATE_C + j * MXU_N, MXU_N)
            y_conf = jnp.dot(actb[...], conf_wo_ref[:, pl.ds(j * MXU_N, MXU_N)],
                             preferred_element_type=F32)
            pbuf[:, cols] = jax.nn.sigmoid(pbuf[:, cols]) * y_conf
        return step

    raw_steps = []
    for j in range(n_tiles):
        raw_steps.append(raw_step(w_conf_ref, 2 * D_MODEL + j * MXU_N, P_CZ + j * MXU_N))
    for j in range(n_tiles):
        raw_steps.append(raw_step(w_gates_ref, j * MXU_N, P_GATE_C + j * MXU_N))
    for j in range(n_tiles):
        raw_steps.append(raw_step(w_gates_ref, D_MODEL + j * MXU_N, P_GATE_G + j * MXU_N))
    for j in range(n_tiles):
        raw_steps.append(raw_step(w_gz_ref, j * MXU_N, P_GZ + j * MXU_N))
    qkv_steps = [qkv_step(j) for j in range(3 * n_tiles)]
    half = ts // 2
    conf_steps = [conf_conv_step(r0, half, c) for r0 in (0, half) for c in range(D_MODEL // LANES)]
    gdn_steps = [gdn_conv_step(part, h) for part in range(3) for h in range(GDN_HEADS)]
    _interleave(qkv_steps + raw_steps, gdn_steps)

    ba = jnp.dot(xbuf[...], w_ba_ref[...], preferred_element_type=F32)
    z = ba + dtb_ref[...]
    softplus = jnp.maximum(z, 0.0) + jnp.log1p(jnp.exp(-jnp.abs(z)))
    g = -jnp.exp(alog_ref[...]) * softplus
    lane = lax.broadcasted_iota(jnp.int32, ba.shape, 1)
    bgb[...] = jnp.where(lane < GDN_HEADS, jax.nn.sigmoid(ba), g)

    pending = (conf_steps + [conf_act_step(r0, ts // 4) for r0 in range(0, ts, ts // 4)]
               + [conf_out_step(j) for j in range(n_tiles)])

    def fill():
        if pending:
            pending.pop(0)()

    C = CHUNK
    row = lax.broadcasted_iota(jnp.int32, (C, C), 0)
    col = lax.broadcasted_iota(jnp.int32, (C, C), 1)
    causal = row >= col
    masks = (causal, row > col, causal.astype(F32), (row == col).astype(F32))
    for c in range(ts // C):
        _delta_rule_chunk(c * C, qkvb, bgb, pbuf, state, obuf, norm_g_ref, masks, fill)
    while pending:
        fill()
    abuf[pl.ds(0, CONF_HALO), :] = abuf[pl.ds(ts, CONF_HALO), :]
    cbuf[pl.ds(0, GDN_HALO), :] = cbuf[pl.ds(ts, GDN_HALO), :]

    y_gdn = _dot(obuf[...], gdn_wo_ref[...])
    hmix = pbuf[:, pl.ds(P_GATE_C, D_MODEL)] + jax.nn.sigmoid(pbuf[:, pl.ds(P_GATE_G, D_MODEL)]) * y_gdn
    sub = _dot(hmix, wo_ref[...])
    out_ref[0] = _layernorm(DN_ALPHA * x_ref[0] + sub, pln_g_ref[...], pln_b_ref[...])


def _const_spec(shape):
    return pl.BlockSpec(shape, lambda b, t: (0,) * len(shape), pipeline_mode=pl.Buffered(1))


def _tile_spec(width):
    return pl.BlockSpec((1, TIME_TILE, width), lambda b, t: (b, t, 0))


def kernel(x, w_in, conf_dw_w, conf_dw_b, conf_ln_g, conf_ln_b, conf_w_out, gdn_conv_w, gdn_A_log,
           gdn_dt_bias, gdn_norm_g, gdn_w_out, w_o, post_ln_g, post_ln_b):
    B, S, D = x.shape
    assert D == D_MODEL and S % TIME_TILE == 0 and TIME_TILE % CHUNK == 0
    H = GDN_HEADS
    o_qkv, o_gz, o_ba, o_gates = 3 * D, 6 * D, 7 * D, 7 * D + 2 * H
    w_conf = w_in[:, :o_qkv].astype(BF16)
    w_qkv = w_in[:, o_qkv:o_gz].astype(BF16)
    w_gz = w_in[:, o_gz:o_ba].astype(BF16)
    w_ba = jnp.pad(w_in[:, o_ba:o_gates], ((0, 0), (0, LANES - 2 * H))).astype(BF16)
    w_gates = w_in[:, o_gates:].astype(BF16)
    dw_w = jnp.pad(conf_dw_w, ((0, CONF_HALO - CONF_KERNEL), (0, 0)))
    gconv_w = jnp.pad(gdn_conv_w, ((0, SUBLANES - GDN_CONV), (0, 0)))
    row = lambda v: v.reshape(1, -1).astype(F32)
    alog = jnp.pad(row(gdn_A_log), ((0, 0), (H, LANES - 2 * H)))
    dtb = jnp.pad(row(gdn_dt_bias), ((0, 0), (H, LANES - 2 * H)))

    weights = (w_conf, w_qkv, w_gz, w_ba, w_gates, dw_w, row(conf_dw_b), row(conf_ln_g),
               row(conf_ln_b), conf_w_out.astype(BF16), gconv_w, alog, dtb, row(gdn_norm_g),
               gdn_w_out.astype(BF16), w_o.astype(BF16), row(post_ln_g), row(post_ln_b))
    ts = TIME_TILE
    return pl.pallas_call(
        _layer_kernel,
        grid=(B, S // ts),
        in_specs=[_tile_spec(D)] + [_const_spec(w.shape) for w in weights],
        out_specs=_tile_spec(D),
        out_shape=jax.ShapeDtypeStruct((B, S, D), x.dtype),
        scratch_shapes=[pltpu.VMEM((ts, D), BF16),
                        pltpu.VMEM((ts + CONF_HALO, D), F32),
                        pltpu.VMEM((ts + GDN_HALO, 3 * D), F32),
                        pltpu.VMEM((ts, D), F32),
                        pltpu.VMEM((ts, 4 * D), F32),
                        pltpu.VMEM((ts, 3 * D), F32),
                        pltpu.VMEM((ts, LANES), F32),
                        pltpu.VMEM((ts, D), BF16),
                        pltpu.VMEM((ts, D), F32),
                        pltpu.VMEM((H, GDN_HEAD_DIM, GDN_HEAD_DIM), F32)],
        compiler_params=pltpu.CompilerParams(dimension_semantics=("arbitrary", "arbitrary"),
                                             vmem_limit_bytes=VMEM_LIMIT_BYTES),
        name="layer",
    )(x, *weights)
```

```python
import functools

import jax
import jax.numpy as jnp
from jax import lax
from jax.experimental import pallas as pl
from jax.experimental.pallas import tpu as pltpu

D_MODEL = 1024
CONF_KERNEL = 31
GDN_HEADS = 8
GDN_HEAD_DIM = 128
GDN_CONV = 4
LN_EPS = 1e-5
RMS_EPS = 1e-6
L2_EPS = 1e-6
DN_ALPHA = 2.0 ** 0.25

LANES = 128
SUBLANES = 8
MXU_N = 256
CONF_HALO = 32
GDN_HALO = 8
TIME_TILE = 256
CHUNK = 128
VMEM_LIMIT_BYTES = 60 * 1024 * 1024

BF16 = jnp.bfloat16
F32 = jnp.float32

P_CZ, P_GATE_C, P_GATE_G, P_GZ = (i * D_MODEL for i in range(4))


def _dot(a, b):
    return jnp.dot(a.astype(BF16), b.astype(BF16), preferred_element_type=F32)


def _dot_nt(a, b):
    return lax.dot_general(a.astype(BF16), b.astype(BF16), (((1,), (1,)), ((), ())),
                           preferred_element_type=F32)


def _dot_tn(a, b):
    return lax.dot_general(a.astype(BF16), b.astype(BF16), (((0,), (0,)), ((), ())),
                           preferred_element_type=F32)


def _silu(x):
    return x * jax.nn.sigmoid(x)


def _layernorm(x, g, b):
    mu = jnp.mean(x, axis=-1, keepdims=True)
    xc = x - mu
    var = jnp.mean(xc * xc, axis=-1, keepdims=True)
    return xc * lax.rsqrt(var + LN_EPS) * g + b


def _shift_up(x, r, rows):
    if r == 0:
        return x[:rows]
    return pltpu.roll(x, x.shape[0] - r, axis=0)[:rows]


def _conf_conv(buf_ref, w_ref, row0, rows, col0):
    cols = pl.ds(col0, LANES)
    base = CONF_HALO - (CONF_KERNEL - 1)
    assert row0 % SUBLANES == 0 and rows % SUBLANES == 0
    acc = None
    for r in range(SUBLANES):
        offs = [o for o in range(base, base + CONF_KERNEL) if o % SUBLANES == r]
        if not offs:
            continue
        n = rows if r == 0 else rows + SUBLANES
        part = None
        for o in offs:
            term = buf_ref[pl.ds(row0 + o - r, n), cols] * w_ref[pl.ds(o - base, 1), cols]
            part = term if part is None else part + term
        part = _shift_up(part, r, rows)
        acc = part if acc is None else acc + part
    return acc


def _gdn_conv(buf_ref, w_ref, row0, rows, col0):
    assert GDN_CONV == 4 and GDN_HALO == SUBLANES and row0 % SUBLANES == 0
    cols = pl.ds(col0, LANES)
    w = [w_ref[pl.ds(k, 1), cols] for k in range(GDN_CONV)]
    c_lo = buf_ref[pl.ds(row0, rows + SUBLANES), cols]
    d_lo = pltpu.roll(c_lo, 1, axis=0)
    c, d = c_lo[SUBLANES:], d_lo[SUBLANES:]
    pair = w[1] * c_lo + w[0] * d_lo
    return w[3] * c + w[2] * d + _shift_up(pair, SUBLANES - 2, rows)


def _interleave(mxu_steps, vpu_steps):
    n_m, n_v = len(mxu_steps), len(vpu_steps)
    done_v = 0
    for i, step in enumerate(mxu_steps):
        step()
        upto = ((i + 1) * n_v) // n_m
        for j in range(done_v, upto):
            vpu_steps[j]()
        done_v = upto
    for j in range(done_v, n_v):
        vpu_steps[j]()


def _delta_rule_chunk(r0, qkvb, bgb, pbuf, state, obuf, norm_g_ref, masks, fill):
    C = CHUNK
    causal, strict, tri, eye = masks
    heads = range(GDN_HEADS)
    rows = pl.ds(r0, C)
    bg = bgb[rows, :]
    gc = jnp.dot(tri, bg, preferred_element_type=F32, precision=lax.Precision.HIGHEST)
    gc_t = gc.T
    cols = [pl.ds(h * GDN_HEAD_DIM, GDN_HEAD_DIM) for h in heads]
    qs = [qkvb[rows, pl.ds(h * GDN_HEAD_DIM, GDN_HEAD_DIM)] for h in heads]
    ks = [qkvb[rows, pl.ds(D_MODEL + h * GDN_HEAD_DIM, GDN_HEAD_DIM)] for h in heads]
    vs = [qkvb[rows, pl.ds(2 * D_MODEL + h * GDN_HEAD_DIM, GDN_HEAD_DIM)] for h in heads]
    betas = [bg[:, h:h + 1] for h in heads]
    g_cols = [gc[:, GDN_HEADS + h:GDN_HEADS + h + 1] for h in heads]
    g_rows = [gc_t[GDN_HEADS + h:GDN_HEADS + h + 1, :] for h in heads]
    e_gs = [jnp.exp(g_cols[h]) for h in heads]
    kbs = [ks[h] * betas[h] for h in heads]
    kqs = [_dot_nt(jnp.concatenate([kbs[h], qs[h]], axis=0), ks[h]) for h in heads]
    fill()
    decays = [jnp.exp(jnp.where(causal, g_cols[h] - g_rows[h], 0.0)) for h in heads]
    ls = [jnp.where(strict, kqs[h][:C] * decays[h], 0.0) for h in heads]
    a_intras = [jnp.where(causal, kqs[h][C:] * decays[h], 0.0) for h in heads]
    invs = [eye - ls[h] for h in heads]
    ps = [_dot(ls[h], ls[h]) for h in heads]
    fill()
    n_pow = 2
    while 2 * n_pow < C:
        rs = [_dot(jnp.concatenate([invs[h], ps[h]], axis=0), ps[h]) for h in heads]
        fill()
        invs = [invs[h] + rs[h][:C] for h in heads]
        ps = [rs[h][C:] for h in heads]
        n_pow *= 2
    invs = [invs[h] + _dot(invs[h], ps[h]) for h in heads]
    fill()
    uws = [_dot(invs[h], jnp.concatenate([vs[h] * betas[h], kbs[h] * e_gs[h]], axis=1))
           for h in heads]
    fill()
    ss = [state[h] for h in heads]
    r2s = [_dot(jnp.concatenate([uws[h][:, GDN_HEAD_DIM:], qs[h] * e_gs[h]], axis=0), ss[h])
           for h in heads]
    fill()
    v_news = [uws[h][:, :GDN_HEAD_DIM] - r2s[h][:C] for h in heads]
    os_ = [r2s[h][C:] + _dot(a_intras[h], v_news[h]) for h in heads]
    fill()
    g_lasts = [g_cols[h][C - 1:C, :] for h in heads]
    k_decs = [ks[h] * jnp.exp(g_lasts[h] - g_cols[h]) for h in heads]
    for h in heads:
        state[h] = ss[h] * jnp.exp(g_lasts[h]) + _dot_tn(k_decs[h], v_news[h])
    fill()
    for h in heads:
        o = os_[h]
        o = o * lax.rsqrt(jnp.mean(o * o, axis=-1, keepdims=True) + RMS_EPS) * norm_g_ref[...]
        obuf[rows, cols[h]] = o * _silu(pbuf[rows, pl.ds(P_GZ + h * GDN_HEAD_DIM, GDN_HEAD_DIM)])


def _layer_kernel(tiles_per_row, x_ref, xprev_ref, w_conf_ref, w_qkv_ref, w_gz_ref, w_ba_ref, w_gates_ref,
                  dw_w_ref, dw_b_ref, ln_g_ref, ln_b_ref, conf_wo_ref, gconv_w_ref,
                  alog_ref, dtb_ref, norm_g_ref, gdn_wo_ref, wo_ref, pln_g_ref, pln_b_ref,
                  out_ref,
                  xbuf, abuf, cbuf, convb, pbuf, qkvb, bgb, actb, obuf, state):
    ts = x_ref.shape[1]
    s = pl.program_id(0)
    t = s % tiles_per_row
    n_tiles = D_MODEL // MXU_N

    @pl.when(s == 0)
    def _():
        obuf[...] = jnp.zeros(obuf.shape, F32)
        pbuf[...] = jnp.zeros(pbuf.shape, F32)

    @pl.when(t == 0)
    def _():
        abuf[pl.ds(0, CONF_HALO), :] = jnp.zeros((CONF_HALO, D_MODEL), F32)
        cbuf[pl.ds(0, GDN_HALO), :] = jnp.zeros((GDN_HALO, 3 * D_MODEL), F32)
        state[...] = jnp.zeros(state.shape, F32)

    y_gdn = _dot(obuf[...], gdn_wo_ref[...])
    hmix = pbuf[:, pl.ds(P_GATE_C, D_MODEL)] + jax.nn.sigmoid(pbuf[:, pl.ds(P_GATE_G, D_MODEL)]) * y_gdn
    sub = _dot(hmix, wo_ref[...])
    out_ref[0] = _layernorm(DN_ALPHA * xprev_ref[0] + sub, pln_g_ref[...], pln_b_ref[...])

    xbuf[...] = x_ref[0].astype(BF16)

    def proj(w_ref, wcol):
        return jnp.dot(xbuf[...], w_ref[:, pl.ds(wcol, MXU_N)], preferred_element_type=F32)

    for j in range(n_tiles):
        c_val = proj(w_conf_ref, j * MXU_N)
        c_glu = proj(w_conf_ref, D_MODEL + j * MXU_N)
        abuf[pl.ds(CONF_HALO, ts), pl.ds(j * MXU_N, MXU_N)] = c_val * jax.nn.sigmoid(c_glu)

    def qkv_step(j):
        def step():
            cbuf[pl.ds(GDN_HALO, ts), pl.ds(j * MXU_N, MXU_N)] = proj(w_qkv_ref, j * MXU_N)
        return step

    def raw_step(w_ref, wcol, pcol):
        def step():
            pbuf[:, pl.ds(pcol, MXU_N)] = proj(w_ref, wcol)
        return step

    def conf_conv_step(row0, rows, c):
        def step():
            cols = pl.ds(c * LANES, LANES)
            convb[pl.ds(row0, rows), cols] = (_conf_conv(abuf, dw_w_ref, row0, rows, c * LANES)
                                              + dw_b_ref[:, cols])
        return step

    def gdn_conv_step(part, h):
        def step():
            col0 = part * D_MODEL + h * GDN_HEAD_DIM
            y = _silu(_gdn_conv(cbuf, gconv_w_ref, 0, ts, col0))
            if part < 2:
                y = y * lax.rsqrt(jnp.sum(y * y, axis=-1, keepdims=True) + L2_EPS)
                if part == 0:
                    y = y * (GDN_HEAD_DIM ** -0.5)
            qkvb[:, pl.ds(col0, GDN_HEAD_DIM)] = y
        return step

    def conf_act_step(row0, rows):
        def step():
            r = pl.ds(row0, rows)
            act = _silu(_layernorm(convb[r, :], ln_g_ref[...], ln_b_ref[...]))
            actb[r, :] = (act * _silu(pbuf[r, pl.ds(P_CZ, D_MODEL)])).astype(BF16)
        return step

    def conf_out_step(j):
        def step():
            cols = pl.ds(P_GATE_C + j * MXU_N, MXU_N)
            y_conf = jnp.dot(actb[...], conf_wo_ref[:, pl.ds(j * MXU_N, MXU_N)],
                             preferred_element_type=F32)
            pbuf[:, cols] = jax.nn.sigmoid(pbuf[:, cols]) * y_conf
        return step

    raw_steps = []
    for j in range(n_tiles):
        raw_steps.append(raw_step(w_conf_ref, 2 * D_MODEL + j * MXU_N, P_CZ + j * MXU_N))
    for j in range(n_tiles):
        raw_steps.append(raw_step(w_gates_ref, j * MXU_N, P_GATE_C + j * MXU_N))
    for j in range(n_tiles):
        raw_steps.append(raw_step(w_gates_ref, D_MODEL + j * MXU_N, P_GATE_G + j * MXU_N))
    for j in range(n_tiles):
        raw_steps.append(raw_step(w_gz_ref, j * MXU_N, P_GZ + j * MXU_N))
    qkv_steps = [qkv_step(j) for j in range(3 * n_tiles)]
    half = ts // 2
    conf_steps = [conf_conv_step(r0, half, c) for r0 in (0, half) for c in range(D_MODEL // LANES)]
    gdn_steps = [gdn_conv_step(part, h) for part in range(3) for h in range(GDN_HEADS)]
    _interleave(qkv_steps + raw_steps, gdn_steps)

    ba = jnp.dot(xbuf[...], w_ba_ref[...], preferred_element_type=F32)
    z = ba + dtb_ref[...]
    softplus = jnp.maximum(z, 0.0) + jnp.log1p(jnp.exp(-jnp.abs(z)))
    g = -jnp.exp(alog_ref[...]) * softplus
    lane = lax.broadcasted_iota(jnp.int32, ba.shape, 1)
    bgb[...] = jnp.where(lane < GDN_HEADS, jax.nn.sigmoid(ba), g)

    pending = (conf_steps + [conf_act_step(r0, ts // 4) for r0 in range(0, ts, ts // 4)]
               + [conf_out_step(j) for j in range(n_tiles)])

    def fill():
        if pending:
            pending.pop(0)()

    C = CHUNK
    row = lax.broadcasted_iota(jnp.int32, (C, C), 0)
    col = lax.broadcasted_iota(jnp.int32, (C, C), 1)
    causal = row >= col
    masks = (causal, row > col, causal.astype(F32), (row == col).astype(F32))
    for c in range(ts // C):
        _delta_rule_chunk(c * C, qkvb, bgb, pbuf, state, obuf, norm_g_ref, masks, fill)
    while pending:
        fill()
    abuf[pl.ds(0, CONF_HALO), :] = abuf[pl.ds(ts, CONF_HALO), :]
    cbuf[pl.ds(0, GDN_HALO), :] = cbuf[pl.ds(ts, GDN_HALO), :]


def _const_spec(shape):
    return pl.BlockSpec(shape, lambda s: (0,) * len(shape), pipeline_mode=pl.Buffered(1))


def kernel(x, w_in, conf_dw_w, conf_dw_b, conf_ln_g, conf_ln_b, conf_w_out, gdn_conv_w, gdn_A_log,
           gdn_dt_bias, gdn_norm_g, gdn_w_out, w_o, post_ln_g, post_ln_b):
    B, S, D = x.shape
    assert D == D_MODEL and S % TIME_TILE == 0 and TIME_TILE % CHUNK == 0
    H = GDN_HEADS
    o_qkv, o_gz, o_ba, o_gates = 3 * D, 6 * D, 7 * D, 7 * D + 2 * H
    w_conf = w_in[:, :o_qkv].astype(BF16)
    w_qkv = w_in[:, o_qkv:o_gz].astype(BF16)
    w_gz = w_in[:, o_gz:o_ba].astype(BF16)
    w_ba = jnp.pad(w_in[:, o_ba:o_gates], ((0, 0), (0, LANES - 2 * H))).astype(BF16)
    w_gates = w_in[:, o_gates:].astype(BF16)
    dw_w = jnp.pad(conf_dw_w, ((0, CONF_HALO - CONF_KERNEL), (0, 0)))
    gconv_w = jnp.pad(gdn_conv_w, ((0, SUBLANES - GDN_CONV), (0, 0)))
    row = lambda v: v.reshape(1, -1).astype(F32)
    alog = jnp.pad(row(gdn_A_log), ((0, 0), (H, LANES - 2 * H)))
    dtb = jnp.pad(row(gdn_dt_bias), ((0, 0), (H, LANES - 2 * H)))

    weights = (w_conf, w_qkv, w_gz, w_ba, w_gates, dw_w, row(conf_dw_b), row(conf_ln_g),
               row(conf_ln_b), conf_w_out.astype(BF16), gconv_w, alog, dtb, row(gdn_norm_g),
               gdn_w_out.astype(BF16), w_o.astype(BF16), row(post_ln_g), row(post_ln_b))
    ts = TIME_TILE
    nt = S // ts
    last = B * nt - 1

    def cur(s):
        g = jnp.minimum(s, last)
        return g // nt, g % nt, 0

    def prev(s):
        g = jnp.maximum(s - 1, 0)
        return g // nt, g % nt, 0

    return pl.pallas_call(
        functools.partial(_layer_kernel, nt),
        grid=(B * nt + 1,),
        in_specs=[pl.BlockSpec((1, ts, D), cur), pl.BlockSpec((1, ts, D), prev)]
                 + [_const_spec(w.shape) for w in weights],
        out_specs=pl.BlockSpec((1, ts, D), prev),
        out_shape=jax.ShapeDtypeStruct((B, S, D), x.dtype),
        scratch_shapes=[pltpu.VMEM((ts, D), BF16),
                        pltpu.VMEM((ts + CONF_HALO, D), F32),
                        pltpu.VMEM((ts + GDN_HALO, 3 * D), F32),
                        pltpu.VMEM((ts, D), F32),
                        pltpu.VMEM((ts, 4 * D), F32),
                        pltpu.VMEM((ts, 3 * D), F32),
                        pltpu.VMEM((ts, LANES), F32),
                        pltpu.VMEM((ts, D), BF16),
                        pltpu.VMEM((ts, D), F32),
                        pltpu.VMEM((H, GDN_HEAD_DIM, GDN_HEAD_DIM), F32)],
        compiler_params=pltpu.CompilerParams(dimension_semantics=("arbitrary",),
                                             vmem_limit_bytes=VMEM_LIMIT_BYTES),
        name="layer",
    )(x, x, *weights)
```

```python
import functools

import jax
import jax.numpy as jnp
from jax import lax
from jax.experimental import pallas as pl
from jax.experimental.pallas import tpu as pltpu

D_MODEL = 1024
CONF_KERNEL = 31
GDN_HEADS = 8
GDN_HEAD_DIM = 128
GDN_CONV = 4
LN_EPS = 1e-5
RMS_EPS = 1e-6
L2_EPS = 1e-6
DN_ALPHA = 2.0 ** 0.25

LANES = 128
SUBLANES = 8
MXU_N = 256
CONF_HALO = 32
GDN_HALO = 8
TIME_TILE = 256
CHUNK = 128
VMEM_LIMIT_BYTES = 60 * 1024 * 1024

BF16 = jnp.bfloat16
F32 = jnp.float32

P_CZ, P_GATE_C, P_GATE_G, P_GZ = (i * D_MODEL for i in range(4))


def _dot(a, b):
    return jnp.dot(a.astype(BF16), b.astype(BF16), preferred_element_type=F32)


def _dot_nt(a, b):
    return lax.dot_general(a.astype(BF16), b.astype(BF16), (((1,), (1,)), ((), ())),
                           preferred_element_type=F32)


def _dot_tn(a, b):
    return lax.dot_general(a.astype(BF16), b.astype(BF16), (((0,), (0,)), ((), ())),
                           preferred_element_type=F32)


def _silu(x):
    return x * jax.nn.sigmoid(x)


def _layernorm(x, g, b):
    mu = jnp.mean(x, axis=-1, keepdims=True)
    xc = x - mu
    var = jnp.mean(xc * xc, axis=-1, keepdims=True)
    return xc * lax.rsqrt(var + LN_EPS) * g + b


def _shift_up(x, r, rows):
    if r == 0:
        return x[:rows]
    return pltpu.roll(x, x.shape[0] - r, axis=0)[:rows]


def _conf_conv(buf_ref, w_ref, row0, rows, col0):
    cols = pl.ds(col0, LANES)
    base = CONF_HALO - (CONF_KERNEL - 1)
    assert row0 % SUBLANES == 0 and rows % SUBLANES == 0
    acc = None
    for r in range(SUBLANES):
        offs = [o for o in range(base, base + CONF_KERNEL) if o % SUBLANES == r]
        if not offs:
            continue
        n = rows if r == 0 else rows + SUBLANES
        part = None
        for o in offs:
            term = buf_ref[pl.ds(row0 + o - r, n), cols] * w_ref[pl.ds(o - base, 1), cols]
            part = term if part is None else part + term
        part = _shift_up(part, r, rows)
        acc = part if acc is None else acc + part
    return acc


def _gdn_conv(buf_ref, w_ref, row0, rows, col0):
    assert GDN_CONV == 4 and GDN_HALO == SUBLANES and row0 % SUBLANES == 0
    cols = pl.ds(col0, LANES)
    w = [w_ref[pl.ds(k, 1), cols] for k in range(GDN_CONV)]
    c_lo = buf_ref[pl.ds(row0, rows + SUBLANES), cols]
    d_lo = pltpu.roll(c_lo, 1, axis=0)
    c, d = c_lo[SUBLANES:], d_lo[SUBLANES:]
    pair = w[1] * c_lo + w[0] * d_lo
    return w[3] * c + w[2] * d + _shift_up(pair, SUBLANES - 2, rows)


def _interleave(mxu_steps, vpu_steps):
    n_m, n_v = len(mxu_steps), len(vpu_steps)
    done_v = 0
    for i, step in enumerate(mxu_steps):
        step()
        upto = ((i + 1) * n_v) // n_m
        for j in range(done_v, upto):
            vpu_steps[j]()
        done_v = upto
    for j in range(done_v, n_v):
        vpu_steps[j]()


def _delta_rule_chunk(r0, qkvb, bgb, pbuf, state, obuf, norm_g_ref, masks, fill):
    C = CHUNK
    causal, strict, tri, eye = masks
    heads = range(GDN_HEADS)
    rows = pl.ds(r0, C)
    bg = bgb[rows, :]
    gc = jnp.dot(tri, bg, preferred_element_type=F32, precision=lax.Precision.HIGHEST)
    gc_t = gc.T
    cols = [pl.ds(h * GDN_HEAD_DIM, GDN_HEAD_DIM) for h in heads]
    qs = [qkvb[rows, pl.ds(h * GDN_HEAD_DIM, GDN_HEAD_DIM)] for h in heads]
    ks = [qkvb[rows, pl.ds(D_MODEL + h * GDN_HEAD_DIM, GDN_HEAD_DIM)] for h in heads]
    vs = [qkvb[rows, pl.ds(2 * D_MODEL + h * GDN_HEAD_DIM, GDN_HEAD_DIM)] for h in heads]
    betas = [bg[:, h:h + 1] for h in heads]
    g_cols = [gc[:, GDN_HEADS + h:GDN_HEADS + h + 1] for h in heads]
    g_rows = [gc_t[GDN_HEADS + h:GDN_HEADS + h + 1, :] for h in heads]
    e_gs = [jnp.exp(g_cols[h]) for h in heads]
    kbs = [ks[h] * betas[h] for h in heads]
    kqs = [_dot_nt(jnp.concatenate([kbs[h], qs[h]], axis=0), ks[h]) for h in heads]
    fill()
    decays = [jnp.exp(jnp.where(causal, g_cols[h] - g_rows[h], 0.0)) for h in heads]
    ls = [jnp.where(strict, kqs[h][:C] * decays[h], 0.0) for h in heads]
    a_intras = [jnp.where(causal, kqs[h][C:] * decays[h], 0.0) for h in heads]
    invs = [eye - ls[h] for h in heads]
    ps = [_dot(ls[h], ls[h]) for h in heads]
    fill()
    n_pow = 2
    while 2 * n_pow < C:
        rs = [_dot(jnp.concatenate([invs[h], ps[h]], axis=0), ps[h]) for h in heads]
        fill()
        invs = [invs[h] + rs[h][:C] for h in heads]
        ps = [rs[h][C:] for h in heads]
        n_pow *= 2
    invs = [invs[h] + _dot(invs[h], ps[h]) for h in heads]
    fill()
    uws = [_dot(invs[h], jnp.concatenate([vs[h] * betas[h], kbs[h] * e_gs[h]], axis=1))
           for h in heads]
    fill()
    ss = [state[h] for h in heads]
    r2s = [_dot(jnp.concatenate([uws[h][:, GDN_HEAD_DIM:], qs[h] * e_gs[h]], axis=0), ss[h])
           for h in heads]
    fill()
    v_news = [uws[h][:, :GDN_HEAD_DIM] - r2s[h][:C] for h in heads]
    os_ = [r2s[h][C:] + _dot(a_intras[h], v_news[h]) for h in heads]
    fill()
    g_lasts = [g_cols[h][C - 1:C, :] for h in heads]
    k_decs = [ks[h] * jnp.exp(g_lasts[h] - g_cols[h]) for h in heads]
    for h in heads:
        state[h] = ss[h] * jnp.exp(g_lasts[h]) + _dot_tn(k_decs[h], v_news[h])
    fill()
    for h in heads:
        o = os_[h]
        o = o * lax.rsqrt(jnp.mean(o * o, axis=-1, keepdims=True) + RMS_EPS) * norm_g_ref[...]
        obuf[rows, cols[h]] = o * _silu(pbuf[rows, pl.ds(P_GZ + h * GDN_HEAD_DIM, GDN_HEAD_DIM)])


def _layer_kernel(tiles_per_row, x_ref, xprev_ref, w_conf_ref, w_qkv_ref, w_gz_ref, w_ba_ref, w_gates_ref,
                  dw_w_ref, dw_b_ref, ln_g_ref, ln_b_ref, conf_wo_ref, gconv_w_ref,
                  alog_ref, dtb_ref, norm_g_ref, gdn_wo_ref, wo_ref, pln_g_ref, pln_b_ref,
                  out_ref,
                  xbuf, abuf, cbuf, convb, pbuf, qkvb, bgb, actb, obuf, state):
    ts = x_ref.shape[1]
    s = pl.program_id(0)
    t = s % tiles_per_row
    n_tiles = D_MODEL // MXU_N

    @pl.when(s == 0)
    def _():
        obuf[...] = jnp.zeros(obuf.shape, F32)
        pbuf[...] = jnp.zeros(pbuf.shape, F32)

    @pl.when(t == 0)
    def _():
        abuf[pl.ds(0, CONF_HALO), :] = jnp.zeros((CONF_HALO, D_MODEL), F32)
        cbuf[pl.ds(0, GDN_HALO), :] = jnp.zeros((GDN_HALO, 3 * D_MODEL), F32)
        state[...] = jnp.zeros(state.shape, F32)


    xbuf[...] = x_ref[0].astype(BF16)

    def proj(w_ref, wcol):
        return jnp.dot(xbuf[...], w_ref[:, pl.ds(wcol, MXU_N)], preferred_element_type=F32)

    for j in range(n_tiles):
        c_val = proj(w_conf_ref, j * MXU_N)
        c_glu = proj(w_conf_ref, D_MODEL + j * MXU_N)
        abuf[pl.ds(CONF_HALO, ts), pl.ds(j * MXU_N, MXU_N)] = c_val * jax.nn.sigmoid(c_glu)

    def qkv_step(j):
        def step():
            cbuf[pl.ds(GDN_HALO, ts), pl.ds(j * MXU_N, MXU_N)] = proj(w_qkv_ref, j * MXU_N)
        return step

    def raw_step(w_ref, wcol, pcol):
        def step():
            pbuf[:, pl.ds(pcol, MXU_N)] = proj(w_ref, wcol)
        return step

    def conf_conv_step(row0, rows, c):
        def step():
            cols = pl.ds(c * LANES, LANES)
            convb[pl.ds(row0, rows), cols] = (_conf_conv(abuf, dw_w_ref, row0, rows, c * LANES)
                                              + dw_b_ref[:, cols])
        return step

    def gdn_conv_step(part, h):
        def step():
            col0 = part * D_MODEL + h * GDN_HEAD_DIM
            y = _silu(_gdn_conv(cbuf, gconv_w_ref, 0, ts, col0))
            if part < 2:
                y = y * lax.rsqrt(jnp.sum(y * y, axis=-1, keepdims=True) + L2_EPS)
                if part == 0:
                    y = y * (GDN_HEAD_DIM ** -0.5)
            qkvb[:, pl.ds(col0, GDN_HEAD_DIM)] = y
        return step

    def conf_act_step(row0, rows):
        def step():
            r = pl.ds(row0, rows)
            act = _silu(_layernorm(convb[r, :], ln_g_ref[...], ln_b_ref[...]))
            actb[r, :] = (act * _silu(pbuf[r, pl.ds(P_CZ, D_MODEL)])).astype(BF16)
        return step

    def conf_out_step(j):
        def step():
            cols = pl.ds(P_GATE_C + j * MXU_N, MXU_N)
            y_conf = jnp.dot(actb[...], conf_wo_ref[:, pl.ds(j * MXU_N, MXU_N)],
                             preferred_element_type=F32)
            pbuf[:, cols] = jax.nn.sigmoid(pbuf[:, cols]) * y_conf
        return step

    raw_steps = []
    for j in range(n_tiles):
        raw_steps.append(raw_step(w_conf_ref, 2 * D_MODEL + j * MXU_N, P_CZ + j * MXU_N))
    for j in range(n_tiles):
        raw_steps.append(raw_step(w_gates_ref, j * MXU_N, P_GATE_C + j * MXU_N))
    for j in range(n_tiles):
        raw_steps.append(raw_step(w_gates_ref, D_MODEL + j * MXU_N, P_GATE_G + j * MXU_N))
    for j in range(n_tiles):
        raw_steps.append(raw_step(w_gz_ref, j * MXU_N, P_GZ + j * MXU_N))
    qkv_steps = [qkv_step(j) for j in range(3 * n_tiles)]
    half = ts // 2
    conf_steps = [conf_conv_step(r0, half, c) for r0 in (0, half) for c in range(D_MODEL // LANES)]
    gdn_steps = [gdn_conv_step(part, h) for part in range(3) for h in range(GDN_HEADS)]
    _interleave(qkv_steps, gdn_steps)
    y_gdn = _dot(obuf[...], gdn_wo_ref[...])
    hmix = pbuf[:, pl.ds(P_GATE_C, D_MODEL)] + jax.nn.sigmoid(pbuf[:, pl.ds(P_GATE_G, D_MODEL)]) * y_gdn
    sub = _dot(hmix, wo_ref[...])
    out_ref[0] = _layernorm(DN_ALPHA * xprev_ref[0] + sub, pln_g_ref[...], pln_b_ref[...])
    for step in raw_steps:
        step()

    ba = jnp.dot(xbuf[...], w_ba_ref[...], preferred_element_type=F32)
    z = ba + dtb_ref[...]
    softplus = jnp.maximum(z, 0.0) + jnp.log1p(jnp.exp(-jnp.abs(z)))
    g = -jnp.exp(alog_ref[...]) * softplus
    lane = lax.broadcasted_iota(jnp.int32, ba.shape, 1)
    bgb[...] = jnp.where(lane < GDN_HEADS, jax.nn.sigmoid(ba), g)

    pending = (conf_steps + [conf_act_step(r0, ts // 4) for r0 in range(0, ts, ts // 4)]
               + [conf_out_step(j) for j in range(n_tiles)])

    def fill():
        if pending:
            pending.pop(0)()

    C = CHUNK
    row = lax.broadcasted_iota(jnp.int32, (C, C), 0)
    col = lax.broadcasted_iota(jnp.int32, (C, C), 1)
    causal = row >= col
    masks = (causal, row > col, causal.astype(F32), (row == col).astype(F32))
    for c in range(ts // C):
        _delta_rule_chunk(c * C, qkvb, bgb, pbuf, state, obuf, norm_g_ref, masks, fill)
    while pending:
        fill()
    abuf[pl.ds(0, CONF_HALO), :] = abuf[pl.ds(ts, CONF_HALO), :]
    cbuf[pl.ds(0, GDN_HALO), :] = cbuf[pl.ds(ts, GDN_HALO), :]


def _const_spec(shape):
    return pl.BlockSpec(shape, lambda s: (0,) * len(shape), pipeline_mode=pl.Buffered(1))


def kernel(x, w_in, conf_dw_w, conf_dw_b, conf_ln_g, conf_ln_b, conf_w_out, gdn_conv_w, gdn_A_log,
           gdn_dt_bias, gdn_norm_g, gdn_w_out, w_o, post_ln_g, post_ln_b):
    B, S, D = x.shape
    assert D == D_MODEL and S % TIME_TILE == 0 and TIME_TILE % CHUNK == 0
    H = GDN_HEADS
    o_qkv, o_gz, o_ba, o_gates = 3 * D, 6 * D, 7 * D, 7 * D + 2 * H
    w_conf = w_in[:, :o_qkv].astype(BF16)
    w_qkv = w_in[:, o_qkv:o_gz].astype(BF16)
    w_gz = w_in[:, o_gz:o_ba].astype(BF16)
    w_ba = jnp.pad(w_in[:, o_ba:o_gates], ((0, 0), (0, LANES - 2 * H))).astype(BF16)
    w_gates = w_in[:, o_gates:].astype(BF16)
    dw_w = jnp.pad(conf_dw_w, ((0, CONF_HALO - CONF_KERNEL), (0, 0)))
    gconv_w = jnp.pad(gdn_conv_w, ((0, SUBLANES - GDN_CONV), (0, 0)))
    row = lambda v: v.reshape(1, -1).astype(F32)
    alog = jnp.pad(row(gdn_A_log), ((0, 0), (H, LANES - 2 * H)))
    dtb = jnp.pad(row(gdn_dt_bias), ((0, 0), (H, LANES - 2 * H)))

    weights = (w_conf, w_qkv, w_gz, w_ba, w_gates, dw_w, row(conf_dw_b), row(conf_ln_g),
               row(conf_ln_b), conf_w_out.astype(BF16), gconv_w, alog, dtb, row(gdn_norm_g),
               gdn_w_out.astype(BF16), w_o.astype(BF16), row(post_ln_g), row(post_ln_b))
    ts = TIME_TILE
    nt = S // ts
    last = B * nt - 1

    def cur(s):
        g = jnp.minimum(s, last)
        return g // nt, g % nt, 0

    def prev(s):
        g = jnp.maximum(s - 1, 0)
        return g // nt, g % nt, 0

    return pl.pallas_call(
        functools.partial(_layer_kernel, nt),
        grid=(B * nt + 1,),
        in_specs=[pl.BlockSpec((1, ts, D), cur), pl.BlockSpec((1, ts, D), prev)]
                 + [_const_spec(w.shape) for w in weights],
        out_specs=pl.BlockSpec((1, ts, D), prev),
        out_shape=jax.ShapeDtypeStruct((B, S, D), x.dtype),
        scratch_shapes=[pltpu.VMEM((ts, D), BF16),
                        pltpu.VMEM((ts + CONF_HALO, D), F32),
                        pltpu.VMEM((ts + GDN_HALO, 3 * D), F32),
                        pltpu.VMEM((ts, D), F32),
                        pltpu.VMEM((ts, 4 * D), F32),
                        pltpu.VMEM((ts, 3 * D), F32),
                        pltpu.VMEM((ts, LANES), F32),
                        pltpu.VMEM((ts, D), BF16),
                        pltpu.VMEM((ts, D), F32),
                        pltpu.VMEM((H, GDN_HEAD_DIM, GDN_HEAD_DIM), F32)],
        compiler_params=pltpu.CompilerParams(dimension_semantics=("arbitrary",),
                                             vmem_limit_bytes=VMEM_LIMIT_BYTES),
        name="layer",
    )(x, x, *weights)
```
